```python
import jax, jax.numpy as jnp
from jax import lax
import numpy as np

D_MODEL = 2048
BATCH = 2
SEQ = 16384
DEPTH = 2

LRU_WIDTH = D_MODEL // 2
LRU_BLOCKS = 8
CONV_WIDTH = 4
LRU_C = 8.0
SB_HEADS = 8
SB_WIDTH = D_MODEL // 2
SB_HEAD_DIM = SB_WIDTH // SB_HEADS
Q_BLOCK = 128
RET_HEADS = 8
RET_QK_DIM = D_MODEL // RET_HEADS
RET_V_DIM = 2 * D_MODEL // RET_HEADS
RET_CHUNK = 128
ROPE_BASE = 10000.0
MOE_GROUPS = 4
MOE_EXPERTS_PER_GROUP = 8
MOE_EXPERTS = MOE_GROUPS * MOE_EXPERTS_PER_GROUP
MOE_TOP_K = 2
EXPERT_FF = 512
MOE_BLOCK = 128
EPS = 1e-6

EVEN_IN = 2 * LRU_WIDTH + 3 * SB_WIDTH
EVEN_SPLITS = [LRU_WIDTH, 2 * LRU_WIDTH, 2 * LRU_WIDTH + SB_WIDTH, 2 * LRU_WIDTH + 2 * SB_WIDTH]
ODD_QK = RET_HEADS * RET_QK_DIM
ODD_V = RET_HEADS * RET_V_DIM
ODD_IN = 2 * ODD_QK + 2 * ODD_V
ODD_SPLITS = [ODD_QK, 2 * ODD_QK, 2 * ODD_QK + ODD_V]

kernel_name = "hybrid_rglru_stickbreak_retention_hmoe"


def rms_norm(x, g):
    xf = x.astype(jnp.float32)
    y = xf * lax.rsqrt(jnp.mean(xf * xf, axis=-1, keepdims=True) + EPS)
    return (y * g.astype(jnp.float32)).astype(x.dtype)


def causal_depthwise_conv(x, w, b):
    S = x.shape[1]
    xp = jnp.pad(x, ((0, 0), (CONV_WIDTH - 1, 0), (0, 0)))
    return b + sum(w[k] * xp[:, k:k + S] for k in range(CONV_WIDTH))


def rg_lru(x, ga_w, ga_b, gx_w, gx_b, lam):
    B, S, C = x.shape
    xf = x.astype(jnp.float32)
    xb = xf.reshape(B, S, LRU_BLOCKS, C // LRU_BLOCKS)
    r = jax.nn.sigmoid(jnp.einsum('bsgi,gij->bsgj', xb, ga_w.astype(jnp.float32)) + ga_b).reshape(B, S, C)
    i = jax.nn.sigmoid(jnp.einsum('bsgi,gij->bsgj', xb, gx_w.astype(jnp.float32)) + gx_b).reshape(B, S, C)
    log_a = -LRU_C * r * jax.nn.softplus(-lam.astype(jnp.float32))
    a = jnp.exp(log_a)
    u = jnp.sqrt(-jnp.expm1(2.0 * log_a)) * (i * xf)

    def combine(left, right):
        a1, b1 = left
        a2, b2 = right
        return a1 * a2, a2 * b1 + b2

    _, h = lax.associative_scan(combine, (a, u), axis=1)
    return h


def stick_breaking_attention(q, k, v):
    B, S, H, dh = q.shape
    nq = S // Q_BLOCK
    scale = dh ** -0.5
    qb = jnp.moveaxis(q.reshape(B, nq, Q_BLOCK, H, dh), 1, 0)
    kf = k.astype(jnp.float32)
    vf = v.astype(jnp.float32)
    key_pos = jnp.arange(S, dtype=jnp.int32)

    def block(args):
        q_blk, blk = args
        z = jnp.einsum('bqhd,bkhd->bhqk', q_blk.astype(jnp.float32), kf) * scale
        q_pos = blk * Q_BLOCK + jnp.arange(Q_BLOCK, dtype=jnp.int32)
        causal = key_pos[None, :] < q_pos[:, None]
        log_beta = jax.nn.log_sigmoid(z)
        log_1m = jnp.where(causal, log_beta - z, 0.0)
        log_rest = lax.cumsum(log_1m, axis=3, reverse=True) - log_1m
        w = jnp.where(causal, jnp.exp(log_beta + log_rest), 0.0)
        return jnp.einsum('bhqk,bkhd->bqhd', w, vf)

    o = lax.map(block, (qb, jnp.arange(nq, dtype=jnp.int32)))
    return jnp.moveaxis(o, 0, 1).reshape(B, S, H * dh)


def rotary(x, pos):
    half = x.shape[-1] // 2
    inv_freq = ROPE_BASE ** (-jnp.arange(half, dtype=jnp.float32) / half)
    ang = pos.astype(jnp.float32)[..., None] * inv_freq
    cos = jnp.cos(ang)[:, :, None, :]
    sin = jnp.sin(ang)[:, :, None, :]
    x1, x2 = x[..., :half], x[..., half:]
    return jnp.concatenate([x1 * cos - x2 * sin, x2 * cos + x1 * sin], axis=-1)


def chunkwise_retention(q, k, v, positions):
    B, S, H, dk = q.shape
    dv = v.shape[-1]
    C = RET_CHUNK
    N = S // C
    q = rotary(q.astype(jnp.float32), positions)
    k = rotary(k.astype(jnp.float32), positions) * (dk ** -0.5)
    v = v.astype(jnp.float32)
    log_g = jnp.log1p(-jnp.power(2.0, -5.0 - jnp.arange(H, dtype=jnp.float32)))
    c = jnp.arange(C, dtype=jnp.float32)
    diff = c[:, None] - c[None, :]
    intra_decay = jnp.where(diff >= 0, jnp.exp(jnp.maximum(diff, 0.0)[None] * log_g[:, None, None]), 0.0)
    q_decay = jnp.exp((c + 1.0)[:, None] * log_g)
    k_decay = jnp.exp((C - 1.0 - c)[:, None] * log_g)
    chunk_decay = jnp.exp(C * log_g)

    qc = q.reshape(B, N, C, H, dk)
    kc = k.reshape(B, N, C, H, dk)
    vc = v.reshape(B, N, C, H, dv)
    s = jnp.einsum('bnchd,bnmhd->bnhcm', qc, kc) * intra_decay
    intra = jnp.einsum('bnhcm,bnmhe->bnche', s, vc)

    def step(R, xs):
        q_n, k_n, v_n = xs
        out = jnp.einsum('bchd,bhde->bche', q_n * q_decay[:, :, None], R)
        R = R * chunk_decay[:, None, None] + jnp.einsum('bchd,bche->bhde', k_n * k_decay[:, :, None], v_n)
        return R, out

    R0 = jnp.zeros((B, H, dk, dv), jnp.float32)
    _, cross = lax.scan(step, R0, (jnp.moveaxis(qc, 1, 0), jnp.moveaxis(kc, 1, 0), jnp.moveaxis(vc, 1, 0)))
    y = intra + jnp.moveaxis(cross, 0, 1)
    return y.reshape(B, S, H, dv)


def head_group_norm(y, gain):
    mu = jnp.mean(y, axis=-1, keepdims=True)
    yc = y - mu
    var = jnp.mean(yc * yc, axis=-1, keepdims=True)
    return yc * lax.rsqrt(var + EPS) * gain.astype(jnp.float32)


def even_mixer(h, w_in, conv_w, conv_b, ga_w, ga_b, gx_w, gx_b, lam, w_out):
    B, S, _ = h.shape
    proj = h @ w_in
    xr, gr, q, k, v = jnp.split(proj, EVEN_SPLITS, axis=-1)
    xr = causal_depthwise_conv(xr, conv_w, conv_b)
    y_a = rg_lru(xr, ga_w, ga_b, gx_w, gx_b, lam) * jax.nn.gelu(gr.astype(jnp.float32))
    shp = (B, S, SB_HEADS, SB_HEAD_DIM)
    y_b = stick_breaking_attention(q.reshape(shp), k.reshape(shp), v.reshape(shp))
    y = jnp.concatenate([y_a, y_b], axis=-1).astype(h.dtype)
    return y @ w_out


def odd_mixer(h, positions, w_in, gn_gain, w_out):
    B, S, _ = h.shape
    proj = h @ w_in
    q, k, v, g = jnp.split(proj, ODD_SPLITS, axis=-1)
    y = chunkwise_retention(q.reshape(B, S, RET_HEADS, RET_QK_DIM), k.reshape(B, S, RET_HEADS, RET_QK_DIM),
                            v.reshape(B, S, RET_HEADS, RET_V_DIM), positions)
    y = head_group_norm(y, gn_gain).reshape(B, S, ODD_V)
    y = jax.nn.silu(g.astype(jnp.float32)) * y
    return y.astype(h.dtype) @ w_out


def hierarchical_moe(h, wg, bg, we, be, w_gate, w_up, w_down):
    B, S, D = h.shape
    T = B * S
    xf = h.reshape(T, D)
    g_logits = (xf @ wg + bg).astype(jnp.float32)
    g_prob = jax.nn.softmax(g_logits, axis=-1)
    g_idx = jnp.argmax(g_logits, axis=-1).astype(jnp.int32)
    g_w = jnp.take_along_axis(g_prob, g_idx[:, None], axis=-1)
    e_logits = (xf @ we + be).astype(jnp.float32).reshape(T, MOE_GROUPS, MOE_EXPERTS_PER_GROUP)
    e_sel = jnp.take_along_axis(e_logits, g_idx[:, None, None], axis=1)[:, 0]
    top_p, top_j = lax.top_k(jax.nn.softmax(e_sel, axis=-1), MOE_TOP_K)
    top_w = top_p / jnp.sum(top_p, axis=-1, keepdims=True) * g_w
    expert = g_idx[:, None] * MOE_EXPERTS_PER_GROUP + top_j.astype(jnp.int32)

    A = T * MOE_TOP_K
    e_flat = expert.reshape(A)
    w_flat = top_w.reshape(A)
    tok_flat = jnp.repeat(jnp.arange(T, dtype=jnp.int32), MOE_TOP_K)
    order = jnp.argsort(e_flat)
    e_sorted = e_flat[order]
    tok_sorted = tok_flat[order]
    w_sorted = w_flat[order]
    counts = jnp.bincount(e_flat, length=MOE_EXPERTS).astype(jnp.int32)
    starts = jnp.cumsum(counts) - counts
    padded = ((counts + MOE_BLOCK - 1) // MOE_BLOCK) * MOE_BLOCK
    pends = jnp.cumsum(padded)
    pstarts = pends - padded
    dest = pstarts[e_sorted] + (jnp.arange(A, dtype=jnp.int32) - starts[e_sorted])
    P = A + MOE_EXPERTS * MOE_BLOCK
    nb = P // MOE_BLOCK
    row_tok = jnp.zeros((P,), jnp.int32).at[dest].set(tok_sorted)
    x_rows = xf[row_tok].reshape(nb, MOE_BLOCK, D)
    blk_expert = jnp.clip(jnp.searchsorted(pends, jnp.arange(nb, dtype=jnp.int32) * MOE_BLOCK, side='right'),
                          0, MOE_EXPERTS - 1).astype(jnp.int32)

    def expert_block(args):
        xb, e = args
        return (jax.nn.silu(xb @ w_gate[e]) * (xb @ w_up[e])) @ w_down[e]

    y_rows = lax.map(expert_block, (x_rows, blk_expert)).reshape(P, D)
    y = y_rows[dest] * w_sorted[:, None].astype(y_rows.dtype)
    out = jax.ops.segment_sum(y, tok_sorted, num_segments=T)
    return out.reshape(B, S, D)


def setup_inputs(seed: int = 0) -> dict:
    key = jax.random.key(seed)
    ks = iter(jax.random.split(key, 40))
    n_even = (DEPTH + 1) // 2
    n_odd = DEPTH // 2
    f32 = jnp.float32

    def nrm(shape, scale):
        return jax.random.normal(next(ks), shape, f32) * scale

    def gain(shape):
        return 1.0 + 0.02 * jax.random.normal(next(ks), shape, f32)

    x = jax.random.normal(next(ks), (BATCH, SEQ, D_MODEL), f32)
    offsets = jax.random.randint(next(ks), (BATCH,), 0, 4096, dtype=jnp.int32)
    positions = offsets[:, None] + jnp.arange(SEQ, dtype=jnp.int32)[None, :]
    blk = LRU_WIDTH // LRU_BLOCKS
    u = jax.random.uniform(next(ks), (n_even, LRU_WIDTH), f32, minval=0.9, maxval=0.999)
    a0 = u ** (1.0 / LRU_C)
    lam = jnp.log(a0) - jnp.log1p(-a0)
    return {
        "x": x,
        "positions": positions,
        "ev_norm": gain((n_even, D_MODEL)),
        "ev_w_in": nrm((n_even, D_MODEL, EVEN_IN), D_MODEL ** -0.5),
        "ev_conv_w": nrm((n_even, CONV_WIDTH, LRU_WIDTH), CONV_WIDTH ** -0.5),
        "ev_conv_b": nrm((n_even, LRU_WIDTH), 0.01),
        "ev_gate_a_w": nrm((n_even, LRU_BLOCKS, blk, blk), blk ** -0.5),
        "ev_gate_a_b": nrm((n_even, LRU_BLOCKS, blk), 0.01),
        "ev_gate_x_w": nrm((n_even, LRU_BLOCKS, blk, blk), blk ** -0.5),
        "ev_gate_x_b": nrm((n_even, LRU_BLOCKS, blk), 0.01),
        "ev_lru_lambda": lam,
        "ev_w_out": nrm((n_even, LRU_WIDTH + SB_WIDTH, D_MODEL), (LRU_WIDTH + SB_WIDTH) ** -0.5),
        "od_norm": gain((n_odd, D_MODEL)),
        "od_w_in": nrm((n_odd, D_MODEL, ODD_IN), D_MODEL ** -0.5),
        "od_gn_gain": gain((n_odd, RET_HEADS, RET_V_DIM)),
        "od_w_out": nrm((n_odd, ODD_V, D_MODEL), ODD_V ** -0.5),
        "ffn_norm": gain((DEPTH, D_MODEL)),
        "router_group_w": nrm((DEPTH, D_MODEL, MOE_GROUPS), D_MODEL ** -0.5),
        "router_group_b": nrm((DEPTH, MOE_GROUPS), 0.01),
        "router_expert_w": nrm((DEPTH, D_MODEL, MOE_EXPERTS), D_MODEL ** -0.5),
        "router_expert_b": nrm((DEPTH, MOE_EXPERTS), 0.01),
        "expert_w_gate": nrm((DEPTH, MOE_EXPERTS, D_MODEL, EXPERT_FF), D_MODEL ** -0.5),
        "expert_w_up": nrm((DEPTH, MOE_EXPERTS, D_MODEL, EXPERT_FF), D_MODEL ** -0.5),
        "expert_w_down": nrm((DEPTH, MOE_EXPERTS, EXPERT_FF, D_MODEL), EXPERT_FF ** -0.5),
        "final_norm": gain((D_MODEL,)),
    }


def reference(x, positions, ev_norm, ev_w_in, ev_conv_w, ev_conv_b, ev_gate_a_w, ev_gate_a_b,
              ev_gate_x_w, ev_gate_x_b, ev_lru_lambda, ev_w_out, od_norm, od_w_in, od_gn_gain,
              od_w_out, ffn_norm, router_group_w, router_group_b, router_expert_w, router_expert_b,
              expert_w_gate, expert_w_up, expert_w_down, final_norm):
    for layer in range(DEPTH):
        i = layer // 2
        if layer % 2 == 0:
            h = rms_norm(x, ev_norm[i])
            x = x + even_mixer(h, ev_w_in[i], ev_conv_w[i], ev_conv_b[i], ev_gate_a_w[i], ev_gate_a_b[i],
                               ev_gate_x_w[i], ev_gate_x_b[i], ev_lru_lambda[i], ev_w_out[i])
        else:
            h = rms_norm(x, od_norm[i])
            x = x + odd_mixer(h, positions, od_w_in[i], od_gn_gain[i], od_w_out[i])
        h = rms_norm(x, ffn_norm[layer])
        x = x + hierarchical_moe(h, router_group_w[layer], router_group_b[layer], router_expert_w[layer],
                                 router_expert_b[layer], expert_w_gate[layer], expert_w_up[layer],
                                 expert_w_down[layer])
    return rms_norm(x, final_norm)
```

```python
import functools

import jax
import jax.numpy as jnp
from jax import lax
from jax.experimental import pallas as pl
from jax.experimental.pallas import tpu as pltpu

F32 = jnp.float32
BF16 = jnp.bfloat16
U32 = jnp.uint32
I32 = jnp.int32

LANES = 128
SUBLANES = 8
VMEM_BYTES_V7X = 64 * 1024 * 1024
VMEM_LIMIT = VMEM_BYTES_V7X * 7 // 8

EPS = 1e-6
LRU_BLOCKS = 8
CONV_WIDTH = 4
LRU_C = 8.0
SB_HEADS = 8
RET_HEADS = 8
ROPE_BASE = 10000.0
MOE_GROUPS = 4
MOE_EXPERTS_PER_GROUP = 8
MOE_EXPERTS = MOE_GROUPS * MOE_EXPERTS_PER_GROUP

F32_EXP_ZERO = -104.0

HI16 = 0xFFFF0000


def _params(*sem):
    return pltpu.CompilerParams(dimension_semantics=sem, vmem_limit_bytes=VMEM_LIMIT)


def _resident(shape):
    return pl.BlockSpec(shape, lambda *_: (0,) * len(shape), pipeline_mode=pl.Buffered(1))


def _pack_bf16_pair(a, b):
    ua = lax.bitcast_convert_type(a.astype(BF16).astype(F32), U32)
    ub = lax.bitcast_convert_type(b.astype(BF16).astype(F32), U32)
    return (ua >> 16) | (ub & jnp.uint32(HI16))


def _unpack_bf16_pair(w):
    a = lax.bitcast_convert_type(w << 16, F32)
    b = lax.bitcast_convert_type(w & jnp.uint32(HI16), F32)
    return a, b


def _norm_matmul_kernel(x_ref, g_ref, w_ref, o_ref, h_ref):
    @pl.when(pl.program_id(1) == 0)
    def _():
        x = x_ref[...]
        ms = jnp.mean(x * x, axis=-1, keepdims=True)
        h_ref[...] = (x * lax.rsqrt(ms + EPS) * g_ref[...]).astype(h_ref.dtype)

    o_ref[...] = jnp.dot(h_ref[...], w_ref[...], preferred_element_type=F32).astype(o_ref.dtype)


def norm_matmul(x, g, w, *, tm, tn):
    t, d = x.shape
    n = w.shape[1]
    return pl.pallas_call(
        _norm_matmul_kernel,
        grid=(t // tm, n // tn),
        in_specs=[
            pl.BlockSpec((tm, d), lambda i, j: (i, 0)),
            pl.BlockSpec((1, d), lambda i, j: (0, 0)),
            pl.BlockSpec((d, tn), lambda i, j: (0, j)),
        ],
        out_specs=pl.BlockSpec((tm, tn), lambda i, j: (i, j)),
        out_shape=jax.ShapeDtypeStruct((t, n), BF16),
        scratch_shapes=[pltpu.VMEM((tm, d), BF16)],
        compiler_params=_params("arbitrary", "arbitrary"),
        name="norm_matmul",
    )(x, g.reshape(1, d), w)


def _sigmoid(x):
    return 1.0 / (1.0 + jnp.exp(-x))


def _lru_kernel(xr_ref, gr_ref, cw_ref, cb_ref, gw_ref, gb_ref, lam_ref, o_ref, tail_ref, h_ref, *, tc):
    @pl.when(pl.program_id(1) == 0)
    def _():
        tail_ref[...] = jnp.zeros_like(tail_ref)
        h_ref[...] = jnp.zeros_like(h_ref)

    c = xr_ref.shape[1]
    bw = c // LRU_BLOCKS
    x = xr_ref[...].astype(F32)
    xe = jnp.concatenate([tail_ref[...], x], axis=0)
    cw = cw_ref[...]
    xc = cb_ref[...] + cw[3:4] * x
    for k in range(CONV_WIDTH - 1):
        off = SUBLANES - (CONV_WIDTH - 1 - k)
        xc = xc + cw[k:k + 1] * xe[off:off + tc]
    tail_ref[...] = x[tc - SUBLANES:, :]

    xcb = xc.astype(BF16)
    parts = [jnp.dot(xcb[:, g * bw:(g + 1) * bw], gw_ref[g], preferred_element_type=F32)
             for g in range(LRU_BLOCKS)]
    gb = gb_ref[...]
    r = _sigmoid(jnp.concatenate([p[:, :bw] for p in parts], axis=1) + gb[0:1])
    ig = _sigmoid(jnp.concatenate([p[:, bw:] for p in parts], axis=1) + gb[1:2])

    nlam = -lam_ref[...]
    softplus_nlam = jnp.maximum(nlam, 0.0) + jnp.log1p(jnp.exp(-jnp.abs(nlam)))
    log_a = (-LRU_C) * r * softplus_nlam
    a = jnp.exp(log_a)
    u = jnp.sqrt(1.0 - a * a) * (ig * xc)

    rows = lax.broadcasted_iota(I32, (tc, c), 0)
    d = 1
    while d < tc:
        keep = rows >= d
        a_prev = jnp.where(keep, pltpu.roll(a, d, 0), 1.0)
        u_prev = jnp.where(keep, pltpu.roll(u, d, 0), 0.0)
        u = a * u_prev + u
        a = a * a_prev
        d *= 2
    h = u + a * h_ref[...]
    h_ref[...] = h[tc - 1:tc, :]
    o_ref[...] = (h * jax.nn.gelu(gr_ref[...].astype(F32))).astype(o_ref.dtype)


def lru_branch(proj, conv_w, conv_b, gate_w, gate_b, lam, *, batch, seq, width, tc):
    nchunk = seq // tc
    return pl.pallas_call(
        functools.partial(_lru_kernel, tc=tc),
        grid=(batch, nchunk),
        in_specs=[
            pl.BlockSpec((tc, width), lambda b, i: (b * nchunk + i, 0)),
            pl.BlockSpec((tc, width), lambda b, i: (b * nchunk + i, 1)),
            pl.BlockSpec((CONV_WIDTH, width), lambda b, i: (0, 0)),
            pl.BlockSpec((1, width), lambda b, i: (0, 0)),
            pl.BlockSpec(gate_w.shape, lambda b, i: (0, 0, 0)),
            pl.BlockSpec((2, width), lambda b, i: (0, 0)),
            pl.BlockSpec((1, width), lambda b, i: (0, 0)),
        ],
        out_specs=pl.BlockSpec((tc, width), lambda b, i: (b * nchunk + i, 0)),
        out_shape=jax.ShapeDtypeStruct((batch * seq, width), BF16),
        scratch_shapes=[pltpu.VMEM((SUBLANES, width), F32), pltpu.VMEM((1, width), F32)],
        compiler_params=_params("arbitrary", "arbitrary"),
        name="lru",
    )(proj, proj, conv_w, conv_b.reshape(1, width), gate_w, gate_b, lam.reshape(1, width))


def _sba_kernel(q_ref, k_ref, v_ref, o_ref, acc_ref, run_ref, *, qb, scale):
    i = pl.program_id(2)
    q = q_ref[...]
    row = lax.broadcasted_iota(I32, (qb, qb), 0)
    col = lax.broadcasted_iota(I32, (qb, qb), 1)
    later = jnp.where(row > col, 1.0, 0.0).astype(BF16)
    causal = col < row

    acc_ref[...] = jnp.zeros_like(acc_ref)
    run_ref[...] = jnp.zeros_like(run_ref)

    def step(j, diagonal):
        start = pl.multiple_of(j * qb, qb)
        kb = k_ref[pl.ds(start, qb), :]
        vb = v_ref[pl.ds(start, qb), :]
        z = lax.dot_general(q, kb, (((1,), (1,)), ((), ())), preferred_element_type=F32) * scale
        softplus_z = jnp.maximum(z, 0.0) + jnp.log(1.0 + jnp.exp(-jnp.abs(z)))
        log_1m = -softplus_z
        if diagonal:
            log_1m = jnp.where(causal, log_1m, 0.0)
        hi = log_1m.astype(BF16)
        r1 = log_1m - hi.astype(F32)
        mid = r1.astype(BF16)
        lo = (r1 - mid.astype(F32)).astype(BF16)
        sums = jnp.dot(jnp.concatenate([hi, mid, lo], axis=0), later, preferred_element_type=F32)
        rest = sums[:qb] + sums[qb:2 * qb] + sums[2 * qb:]
        run = run_ref[...]
        w = jnp.exp((z - softplus_z) + rest + run)
        if diagonal:
            w = jnp.where(causal, w, 0.0)
        acc_ref[...] += jnp.dot(w.astype(BF16), vb, preferred_element_type=F32)
        new_run = run + rest[:, 0:1] + log_1m[:, 0:1]
        run_ref[...] = new_run
        return jnp.max(new_run) > F32_EXP_ZERO

    alive = step(i, True)

    def cond(carry):
        j, alive = carry
        return jnp.logical_and(j >= 0, alive)

    def body(carry):
        j, _ = carry
        return j - 1, step(j, False)

    lax.while_loop(cond, body, (i - 1, alive))
    o_ref[...] = acc_ref[...].astype(o_ref.dtype)


def stick_breaking(proj, *, batch, seq, heads, head_dim, q_col, k_col, v_col, qb):
    nq = seq // qb
    return pl.pallas_call(
        functools.partial(_sba_kernel, qb=qb, scale=head_dim ** -0.5),
        grid=(batch, heads, nq),
        in_specs=[
            pl.BlockSpec((qb, head_dim), lambda b, h, i: (b * nq + i, q_col + h)),
            pl.BlockSpec((seq, head_dim), lambda b, h, i: (b, k_col + h)),
            pl.BlockSpec((seq, head_dim), lambda b, h, i: (b, v_col + h)),
        ],
        out_specs=pl.BlockSpec((qb, head_dim), lambda b, h, i: (b * nq + i, h)),
        out_shape=jax.ShapeDtypeStruct((batch * seq, heads * head_dim), BF16),
        scratch_shapes=[pltpu.VMEM((qb, head_dim), F32), pltpu.VMEM((qb, 1), F32)],
        compiler_params=_params("arbitrary", "arbitrary", "arbitrary"),
        name="sba",
    )(proj, proj, proj)


def _out_router_kernel(*refs, n_in):
    y_refs = refs[:n_in]
    w_refs = refs[n_in:2 * n_in]
    x_ref, g_ref, rwh_ref, rwl_ref, rb_ref = refs[2 * n_in:2 * n_in + 5]
    xo_ref, hp_ref, route_ref, cnt_ref, carry_ref = refs[2 * n_in + 5:]

    @pl.when(pl.program_id(0) == 0)
    def _():
        carry_ref[...] = jnp.zeros_like(carry_ref)

    acc = x_ref[...]
    for y_ref, w_ref in zip(y_refs, w_refs):
        acc = acc + jnp.dot(y_ref[...], w_ref[...], preferred_element_type=F32)
    xo_ref[...] = acc

    tm, d = acc.shape
    ms = jnp.mean(acc * acc, axis=-1, keepdims=True)
    h = acc * lax.rsqrt(ms + EPS) * g_ref[...]
    hp_ref[...] = _pack_bf16_pair(h[:, :d // 2], h[:, d // 2:])

    h_hi = h.astype(BF16)
    h_lo = (h - h_hi.astype(F32)).astype(BF16)
    logits = (jnp.dot(h_hi, rwh_ref[...], preferred_element_type=F32)
              + jnp.dot(h_lo, rwh_ref[...], preferred_element_type=F32)
              + jnp.dot(h_hi, rwl_ref[...], preferred_element_type=F32)
              + rb_ref[...])

    lane = lax.broadcasted_iota(I32, (tm, LANES), 1)
    neg = jnp.float32(-jnp.inf)
    big = jnp.int32(LANES)

    def first_argmax(vals, mask):
        m = jnp.max(jnp.where(mask, vals, neg), axis=-1, keepdims=True)
        idx = jnp.min(jnp.where(jnp.logical_and(mask, vals == m), lane, big), axis=-1, keepdims=True)
        return m, idx

    is_group = lane < MOE_GROUPS
    g_max, g_idx = first_argmax(logits, is_group)
    g_w = 1.0 / jnp.sum(jnp.where(is_group, jnp.exp(logits - g_max), 0.0), axis=-1, keepdims=True)
    e_lo = MOE_GROUPS + g_idx * MOE_EXPERTS_PER_GROUP
    in_group = jnp.logical_and(lane >= e_lo, lane < e_lo + MOE_EXPERTS_PER_GROUP)
    l1, i1 = first_argmax(logits, in_group)
    l2, i2 = first_argmax(logits, jnp.logical_and(in_group, lane != i1))
    t = jnp.exp(l2 - l1)
    w1 = g_w / (1.0 + t)
    w2 = g_w * t / (1.0 + t)
    e1 = i1 - MOE_GROUPS
    e2 = i2 - MOE_GROUPS

    onehot = jnp.where(jnp.logical_or(lane == e1, lane == e2), 1.0, 0.0)
    rr = lax.broadcasted_iota(I32, (tm, tm), 0)
    cc = lax.broadcasted_iota(I32, (tm, tm), 1)
    lower = jnp.where(cc < rr, 1.0, 0.0).astype(BF16)
    before = jnp.dot(lower, onehot.astype(BF16), preferred_element_type=F32) + carry_ref[...]
    rank1 = jnp.sum(jnp.where(lane == e1, before, 0.0), axis=-1, keepdims=True)
    rank2 = jnp.sum(jnp.where(lane == e2, before, 0.0), axis=-1, keepdims=True)
    new_carry = before[tm - 1:tm, :] + onehot[tm - 1:tm, :]
    carry_ref[...] = new_carry
    cnt_ref[...] = jnp.broadcast_to(new_carry, cnt_ref.shape)

    route = jnp.where(lane == 0, e1.astype(F32), 0.0)
    route = jnp.where(lane == 1, e2.astype(F32), route)
    route = jnp.where(lane == 2, w1, route)
    route = jnp.where(lane == 3, w2, route)
    route = jnp.where(lane == 4, rank1, route)
    route = jnp.where(lane == 5, rank2, route)
    route_ref[...] = route


def out_router(ys, ws, x, g, rw_hi, rw_lo, rb, *, tm):
    t, d = x.shape
    n_in = len(ys)
    in_specs = ([pl.BlockSpec((tm, y.shape[1]), lambda i: (i, 0)) for y in ys]
                + [_resident(w.shape) for w in ws]
                + [pl.BlockSpec((tm, d), lambda i: (i, 0)),
                   _resident((1, d)), _resident(rw_hi.shape), _resident(rw_lo.shape), _resident((1, LANES))])
    return pl.pallas_call(
        functools.partial(_out_router_kernel, n_in=n_in),
        grid=(t // tm,),
        in_specs=in_specs,
        out_specs=[
            pl.BlockSpec((tm, d), lambda i: (i, 0)),
            pl.BlockSpec((tm, d // 2), lambda i: (i, 0)),
            pl.BlockSpec((tm, LANES), lambda i: (i, 0)),
            pl.BlockSpec((SUBLANES, LANES), lambda i: (0, 0)),
        ],
        out_shape=[
            jax.ShapeDtypeStruct((t, d), F32),
            jax.ShapeDtypeStruct((t, d // 2), U32),
            jax.ShapeDtypeStruct((t, LANES), F32),
            jax.ShapeDtypeStruct((SUBLANES, LANES), F32),
        ],
        scratch_shapes=[pltpu.VMEM((1, LANES), F32)],
        compiler_params=_params("arbitrary"),
        name="out_router",
    )(*ys, *ws, x, g.reshape(1, d), rw_hi, rw_lo, rb)


def _dispatch_kernel(dest_ref, h_ref, xs_ref, sem, *, tm, top_k):
    base = pl.program_id(0) * tm * top_k

    def row_copy(t, k):
        return pltpu.make_async_copy(h_ref.at[pl.ds(t, 1)],
                                     xs_ref.at[pl.ds(dest_ref[base + t * top_k + k], 1)], sem)

    def issue(t, carry):
        for k in range(top_k):
            row_copy(t, k).start()
        return carry

    lax.fori_loop(0, tm, issue, 0)

    def drain(t, carry):
        for k in range(top_k):
            row_copy(t, k).wait()
        return carry

    lax.fori_loop(0, tm, drain, 0)


def dispatch(dest, h_packed, *, rows_out, tm, top_k):
    t, dw = h_packed.shape
    return pl.pallas_call(
        functools.partial(_dispatch_kernel, tm=tm, top_k=top_k),
        grid_spec=pltpu.PrefetchScalarGridSpec(
            num_scalar_prefetch=1,
            grid=(t // tm,),
            in_specs=[pl.BlockSpec((tm, dw), lambda i, dest: (i, 0))],
            out_specs=pl.BlockSpec(memory_space=pl.ANY),
            scratch_shapes=[pltpu.SemaphoreType.DMA(())],
        ),
        out_shape=jax.ShapeDtypeStruct((rows_out, dw), U32),
        compiler_params=pltpu.CompilerParams(dimension_semantics=("arbitrary",),
                                             vmem_limit_bytes=VMEM_LIMIT, has_side_effects=True),
        name="dispatch",
    )(dest, h_packed)


def _ffn_kernel(blk_e_ref, blk_valid_ref, nused_ref, x_ref, wg_ref, wu_ref, wd_ref, y_ref):
    j = pl.program_id(0)

    @pl.when(j < nused_ref[0])
    def _():
        tb = x_ref.shape[0]
        rows = lax.broadcasted_iota(I32, x_ref.shape, 0)
        packed = jnp.where(rows < blk_valid_ref[j], x_ref[...], jnp.uint32(0))
        a, b = _unpack_bf16_pair(packed)
        x = jnp.concatenate([a.astype(BF16), b.astype(BF16)], axis=1)
        gate = jnp.dot(x, wg_ref[0], preferred_element_type=F32)
        up = jnp.dot(x, wu_ref[0], preferred_element_type=F32)
        hid = (gate * _sigmoid(gate) * up).astype(BF16)
        y = jnp.dot(hid, wd_ref[0], preferred_element_type=F32)
        half = y.shape[1] // 2
        y_ref[...] = _pack_bf16_pair(y[:, :half], y[:, half:])


def expert_ffn(blk_e, blk_valid, nused, xs, w_gate, w_up, w_down, *, tb):
    p, dw = xs.shape
    _, d, ff = w_gate.shape

    def row_map(j, be, bv, nu):
        return (jnp.minimum(j, nu[0] - 1), 0)

    def w_map(j, be, bv, nu):
        return (be[j], 0, 0)

    return pl.pallas_call(
        _ffn_kernel,
        grid_spec=pltpu.PrefetchScalarGridSpec(
            num_scalar_prefetch=3,
            grid=(p // tb,),
            in_specs=[
                pl.BlockSpec((tb, dw), row_map),
                pl.BlockSpec((1, d, ff), w_map),
                pl.BlockSpec((1, d, ff), w_map),
                pl.BlockSpec((1, ff, d), w_map),
            ],
            out_specs=pl.BlockSpec((tb, dw), row_map),
        ),
        out_shape=jax.ShapeDtypeStruct((p, dw), U32),
        compiler_params=_params("arbitrary"),
        name="ffn",
    )(blk_e, blk_valid, nused, xs, w_gate, w_up, w_down)


def _combine_kernel(dest_ref, x_ref, route_ref, g_ref, ys_ref, o_ref, buf_ref, sem, *, tm, top_k, final_norm):
    base = pl.program_id(0) * tm * top_k

    def row_copy(t, k):
        return pltpu.make_async_copy(ys_ref.at[pl.ds(dest_ref[base + t * top_k + k], 1)],
                                     buf_ref.at[k, pl.ds(t, 1)], sem)

    def issue(t, carry):
        for k in range(top_k):
            row_copy(t, k).start()
        return carry

    lax.fori_loop(0, tm, issue, 0)

    def drain(t, carry):
        for k in range(top_k):
            row_copy(t, k).wait()
        return carry

    lax.fori_loop(0, tm, drain, 0)

    route = route_ref[...]
    out = x_ref[...]
    for k in range(top_k):
        a, b = _unpack_bf16_pair(buf_ref[k])
        out = out + route[:, 2 + k:3 + k] * jnp.concatenate([a, b], axis=1)
    if final_norm:
        ms = jnp.mean(out * out, axis=-1, keepdims=True)
        out = out * lax.rsqrt(ms + EPS) * g_ref[...]
    o_ref[...] = out


def combine(dest, x, route, g, ys, *, tm, top_k, final_norm):
    t, d = x.shape
    dw = ys.shape[1]
    return pl.pallas_call(
        functools.partial(_combine_kernel, tm=tm, top_k=top_k, final_norm=final_norm),
        grid_spec=pltpu.PrefetchScalarGridSpec(
            num_scalar_prefetch=1,
            grid=(t // tm,),
            in_specs=[
                pl.BlockSpec((tm, d), lambda i, dest: (i, 0)),
                pl.BlockSpec((tm, LANES), lambda i, dest: (i, 0)),
                pl.BlockSpec((1, d), lambda i, dest: (0, 0)),
                pl.BlockSpec(memory_space=pl.ANY),
            ],
            out_specs=pl.BlockSpec((tm, d), lambda i, dest: (i, 0)),
            scratch_shapes=[pltpu.VMEM((top_k, tm, dw), U32), pltpu.SemaphoreType.DMA(())],
        ),
        out_shape=jax.ShapeDtypeStruct((t, d), F32),
        compiler_params=_params("arbitrary"),
        name="combine",
    )(dest, x, route, g.reshape(1, d), ys)


def moe_layer(x_new, h_packed, route, counts, w_gate, w_up, w_down, g_final, *, tb, tm_rows, final_norm):
    t = x_new.shape[0]
    top_k = 2
    cnt = counts[0, :MOE_EXPERTS].astype(I32)
    padded = ((cnt + tb - 1) // tb) * tb
    pends = jnp.cumsum(padded)
    pstarts = pends - padded
    experts = route[:, 0:top_k].astype(I32)
    ranks = route[:, 4:4 + top_k].astype(I32)
    dest = (pstarts[experts] + ranks).reshape(t * top_k)
    rows_out = t * top_k + MOE_EXPERTS * tb
    nb = rows_out // tb
    blk_start = jnp.arange(nb, dtype=I32) * tb
    blk_e = jnp.clip(jnp.searchsorted(pends, blk_start, side="right"), 0, MOE_EXPERTS - 1).astype(I32)
    blk_valid = jnp.clip(pstarts[blk_e] + cnt[blk_e] - blk_start, 0, tb).astype(I32)
    nused = (pends[-1:] // tb).astype(I32)

    xs = dispatch(dest, h_packed, rows_out=rows_out, tm=tm_rows, top_k=top_k)
    ys = expert_ffn(blk_e, blk_valid, nused, xs, w_gate, w_up, w_down, tb=tb)
    return combine(dest, x_new, route, g_final, ys, tm=tm_rows, top_k=top_k, final_norm=final_norm)


def _rope_kernel(pos_ref, inv_ref, cos_ref, sin_ref):
    ang = pos_ref[...].astype(F32) * inv_ref[...]
    cos_ref[...] = jnp.cos(ang)
    sin_ref[...] = jnp.sin(ang)


def rope_table(positions, half, *, tm):
    t = positions.size
    inv_freq = (ROPE_BASE ** (-jnp.arange(half, dtype=F32) / half)).reshape(1, half)
    return pl.pallas_call(
        _rope_kernel,
        grid=(t // tm,),
        in_specs=[pl.BlockSpec((tm, 1), lambda i: (i, 0)), pl.BlockSpec((1, half), lambda i: (0, 0))],
        out_specs=[pl.BlockSpec((tm, half), lambda i: (i, 0))] * 2,
        out_shape=[jax.ShapeDtypeStruct((t, half), F32)] * 2,
        compiler_params=_params("arbitrary"),
        name="rope_table",
    )(positions.reshape(t, 1), inv_freq)


def _retention_kernel(lg_ref, q_ref, k_ref, v_ref, g_ref, cos_ref, sin_ref, gain_ref, o_ref,
                      state_ref, decay_ref, qd_ref, kd_ref, *, c):
    dk = q_ref.shape[1]
    dv = v_ref.shape[1]
    half = dk // 2
    lg = lg_ref[pl.program_id(1)]

    @pl.when(pl.program_id(2) == 0)
    def _():
        state_ref[...] = jnp.zeros_like(state_ref)
        ri = lax.broadcasted_iota(I32, (c, c), 0)
        ci = lax.broadcasted_iota(I32, (c, c), 1)
        diff = (ri - ci).astype(F32)
        decay_ref[...] = jnp.where(diff >= 0.0, jnp.exp(jnp.maximum(diff, 0.0) * lg), 0.0)
        pos = lax.broadcasted_iota(I32, (c, dk), 0).astype(F32)
        qd_ref[...] = jnp.exp((pos + 1.0) * lg)
        kd_ref[...] = jnp.exp((c - 1.0 - pos) * lg)

    cos = cos_ref[...]
    sin = sin_ref[...]

    def rotate(x):
        x1, x2 = x[:, :half], x[:, half:]
        return jnp.concatenate([x1 * cos - x2 * sin, x2 * cos + x1 * sin], axis=1)

    qr = rotate(q_ref[...].astype(F32))
    kr = rotate(k_ref[...].astype(F32)) * (dk ** -0.5)
    v = v_ref[...]

    s = lax.dot_general(qr.astype(BF16), kr.astype(BF16), (((1,), (1,)), ((), ())),
                        preferred_element_type=F32) * decay_ref[...]
    state = state_ref[...]
    y = (jnp.dot(s.astype(BF16), v, preferred_element_type=F32)
         + jnp.dot((qr * qd_ref[...]).astype(BF16), state.astype(BF16), preferred_element_type=F32))
    chunk_decay = jnp.exp(jnp.zeros((1, dv), F32) + c * lg)
    state_ref[...] = state * chunk_decay + lax.dot_general(
        (kr * kd_ref[...]).astype(BF16), v, (((0,), (0,)), ((), ())), preferred_element_type=F32)

    mu = jnp.mean(y, axis=-1, keepdims=True)
    yc = y - mu
    var = jnp.mean(yc * yc, axis=-1, keepdims=True)
    yn = yc * lax.rsqrt(var + EPS) * gain_ref[0]
    g = g_ref[...].astype(F32)
    o_ref[...] = (g * _sigmoid(g) * yn).astype(o_ref.dtype)


def retention(proj, cos, sin, gain, *, batch, seq, heads, dk, dv, c):
    nchunk = seq // c
    log_g = jnp.log1p(-jnp.power(2.0, -5.0 - jnp.arange(heads, dtype=F32)))
    k_col = heads
    v_col = 2 * heads * dk // dv
    g_col = v_col + heads
    half = dk // 2

    def rows(b, h, n, lg):
        return b * nchunk + n

    return pl.pallas_call(
        functools.partial(_retention_kernel, c=c),
        grid_spec=pltpu.PrefetchScalarGridSpec(
            num_scalar_prefetch=1,
            grid=(batch, heads, nchunk),
            in_specs=[
                pl.BlockSpec((c, dk), lambda b, h, n, lg: (rows(b, h, n, lg), h)),
                pl.BlockSpec((c, dk), lambda b, h, n, lg: (rows(b, h, n, lg), k_col + h)),
                pl.BlockSpec((c, dv), lambda b, h, n, lg: (rows(b, h, n, lg), v_col + h)),
                pl.BlockSpec((c, dv), lambda b, h, n, lg: (rows(b, h, n, lg), g_col + h)),
                pl.BlockSpec((c, half), lambda b, h, n, lg: (rows(b, h, n, lg), 0)),
                pl.BlockSpec((c, half), lambda b, h, n, lg: (rows(b, h, n, lg), 0)),
                pl.BlockSpec((1, 1, dv), lambda b, h, n, lg: (h, 0, 0)),
            ],
            out_specs=pl.BlockSpec((c, dv), lambda b, h, n, lg: (rows(b, h, n, lg), h)),
            scratch_shapes=[
                pltpu.VMEM((dk, dv), F32),
                pltpu.VMEM((c, c), F32),
                pltpu.VMEM((c, dk), F32),
                pltpu.VMEM((c, dk), F32),
            ],
        ),
        out_shape=jax.ShapeDtypeStruct((batch * seq, heads * dv), BF16),
        compiler_params=_params("arbitrary", "arbitrary", "arbitrary"),
        name="retention",
    )(log_g, proj, proj, proj, proj, cos, sin, gain.reshape(heads, 1, dv))


def _router_params(wg, bg, we, be):
    d = wg.shape[0]
    n = MOE_GROUPS + MOE_EXPERTS
    w = jnp.concatenate([wg, we, jnp.zeros((d, LANES - n), F32)], axis=1)
    b = jnp.concatenate([bg, be, jnp.zeros((LANES - n,), F32)]).reshape(1, LANES)
    w_hi = w.astype(BF16)
    w_lo = (w - w_hi.astype(F32)).astype(BF16)
    return w_hi, w_lo, b


def _pick(total, want):
    t = min(want, total)
    while total % t:
        t //= 2
    return t


def kernel(x, positions, ev_norm, ev_w_in, ev_conv_w, ev_conv_b, ev_gate_a_w, ev_gate_a_b, ev_gate_x_w, ev_gate_x_b, ev_lru_lambda, ev_w_out, od_norm, od_w_in, od_gn_gain, od_w_out, ffn_norm, router_group_w, router_group_b, router_expert_w, router_expert_b, expert_w_gate, expert_w_up, expert_w_down, final_norm):
    batch, seq, d = x.shape
    t = batch * seq
    depth = ffn_norm.shape[0]
    lru_w = ev_conv_w.shape[2]
    sb_w = (ev_w_in.shape[2] - 2 * lru_w) // 3
    head_dim = sb_w // SB_HEADS
    dv = od_gn_gain.shape[2]
    dk = (od_w_in.shape[2] - 2 * RET_HEADS * dv) // (2 * RET_HEADS)

    tm_proj = _pick(t, 1024)
    tm_out = _pick(t, 512)
    tm_rows = _pick(t, 512)
    tb = 256

    xf = x.reshape(t, d)
    for layer in range(depth):
        i = layer // 2
        if layer % 2 == 0:
            proj = norm_matmul(xf, ev_norm[i], ev_w_in[i].astype(BF16), tm=tm_proj, tn=_pick(ev_w_in.shape[2], 1024))
            gate_w = jnp.concatenate([ev_gate_a_w[i], ev_gate_x_w[i]], axis=2).astype(BF16)
            gate_b = jnp.stack([ev_gate_a_b[i].reshape(lru_w), ev_gate_x_b[i].reshape(lru_w)])
            y_a = lru_branch(proj, ev_conv_w[i], ev_conv_b[i], gate_w, gate_b, ev_lru_lambda[i],
                             batch=batch, seq=seq, width=lru_w, tc=_pick(seq, 256))
            col0 = 2 * lru_w // head_dim
            y_b = stick_breaking(proj, batch=batch, seq=seq, heads=SB_HEADS, head_dim=head_dim,
                                 q_col=col0, k_col=col0 + SB_HEADS, v_col=col0 + 2 * SB_HEADS,
                                 qb=_pick(seq, 128))
            w_out = ev_w_out[i].astype(BF16)
            ys, ws = [y_a, y_b], [w_out[:lru_w], w_out[lru_w:]]
        else:
            proj = norm_matmul(xf, od_norm[i], od_w_in[i].astype(BF16), tm=tm_proj, tn=_pick(od_w_in.shape[2], 1024))
            cos, sin = rope_table(positions, dk // 2, tm=_pick(t, 1024))
            y = retention(proj, cos, sin, od_gn_gain[i], batch=batch, seq=seq, heads=RET_HEADS,
                          dk=dk, dv=dv, c=_pick(seq, 256))
            ys, ws = [y], [od_w_out[i].astype(BF16)]

        rw_hi, rw_lo, rb = _router_params(router_group_w[layer], router_group_b[layer],
                                          router_expert_w[layer], router_expert_b[layer])
        x_new, h_packed, route, counts = out_router(ys, ws, xf, ffn_norm[layer], rw_hi, rw_lo, rb, tm=tm_out)
        last = layer == depth - 1
        xf = moe_layer(x_new, h_packed, route, counts,
                       expert_w_gate[layer].astype(BF16), expert_w_up[layer].astype(BF16),
                       expert_w_down[layer].astype(BF16), final_norm,
                       tb=tb, tm_rows=tm_rows, final_norm=last)
    return xf.reshape(batch, seq, d)
```

```python
import functools

import jax
import jax.numpy as jnp
from jax import lax
from jax.experimental import pallas as pl
from jax.experimental.pallas import tpu as pltpu

F32 = jnp.float32
BF16 = jnp.bfloat16
U32 = jnp.uint32
I32 = jnp.int32

LANES = 128
SUBLANES = 8
VMEM_BYTES_V7X = 64 * 1024 * 1024
VMEM_LIMIT = VMEM_BYTES_V7X * 7 // 8

EPS = 1e-6
LRU_BLOCKS = 8
CONV_WIDTH = 4
LRU_C = 8.0
SB_HEADS = 8
RET_HEADS = 8
ROPE_BASE = 10000.0
MOE_GROUPS = 4
MOE_EXPERTS_PER_GROUP = 8
MOE_EXPERTS = MOE_GROUPS * MOE_EXPERTS_PER_GROUP

SBA_QUERY_BLOCK = 64
SBA_CHAINS = 16
SBA_FIRST_KEYS = 256
SBA_MORE_KEYS = 128
LOG2E = 1.4426950408889634

DMA_ISSUE_UNROLL = 8

F32_EXP_ZERO = -104.0

HI16 = 0xFFFF0000


def _params(*sem):
    return pltpu.CompilerParams(dimension_semantics=sem, vmem_limit_bytes=VMEM_LIMIT)


def _resident(shape):
    return pl.BlockSpec(shape, lambda *_: (0,) * len(shape), pipeline_mode=pl.Buffered(1))


def _pack_bf16_pair(a, b):
    ua = lax.bitcast_convert_type(a.astype(BF16).astype(F32), U32)
    ub = lax.bitcast_convert_type(b.astype(BF16).astype(F32), U32)
    return (ua >> 16) | (ub & jnp.uint32(HI16))


def _unpack_bf16_pair(w):
    a = lax.bitcast_convert_type(w << 16, F32)
    b = lax.bitcast_convert_type(w & jnp.uint32(HI16), F32)
    return a, b


def _norm_matmul_kernel(x_ref, g_ref, w_ref, o_ref, h_ref):
    @pl.when(pl.program_id(1) == 0)
    def _():
        x = x_ref[...]
        ms = jnp.mean(x * x, axis=-1, keepdims=True)
        h_ref[...] = (x * lax.rsqrt(ms + EPS) * g_ref[...]).astype(h_ref.dtype)

    o_ref[...] = jnp.dot(h_ref[...], w_ref[...], preferred_element_type=F32).astype(o_ref.dtype)


def norm_matmul(x, g, w, *, tm, tn):
    t, d = x.shape
    n = w.shape[1]
    return pl.pallas_call(
        _norm_matmul_kernel,
        grid=(t // tm, n // tn),
        in_specs=[
            pl.BlockSpec((tm, d), lambda i, j: (i, 0)),
            pl.BlockSpec((1, d), lambda i, j: (0, 0)),
            pl.BlockSpec((d, tn), lambda i, j: (0, j)),
        ],
        out_specs=pl.BlockSpec((tm, tn), lambda i, j: (i, j)),
        out_shape=jax.ShapeDtypeStruct((t, n), BF16),
        scratch_shapes=[pltpu.VMEM((tm, d), BF16)],
        compiler_params=_params("arbitrary", "arbitrary"),
        name="norm_matmul",
    )(x, g.reshape(1, d), w)


def _sigmoid(x):
    return 1.0 / (1.0 + jnp.exp(-x))


def _lru_kernel(xr_ref, gr_ref, cw_ref, cb_ref, gw_ref, gb_ref, lam_ref, o_ref, tail_ref, h_ref, *, tc):
    @pl.when(pl.program_id(1) == 0)
    def _():
        tail_ref[...] = jnp.zeros_like(tail_ref)
        h_ref[...] = jnp.zeros_like(h_ref)

    c = xr_ref.shape[1]
    bw = c // LRU_BLOCKS
    x = xr_ref[...].astype(F32)
    xe = jnp.concatenate([tail_ref[...], x], axis=0)
    cw = cw_ref[...]
    xc = cb_ref[...] + cw[3:4] * x
    for k in range(CONV_WIDTH - 1):
        off = SUBLANES - (CONV_WIDTH - 1 - k)
        xc = xc + cw[k:k + 1] * xe[off:off + tc]
    tail_ref[...] = x[tc - SUBLANES:, :]

    xcb = xc.astype(BF16)
    parts = [jnp.dot(xcb[:, g * bw:(g + 1) * bw], gw_ref[g], preferred_element_type=F32)
             for g in range(LRU_BLOCKS)]
    gb = gb_ref[...]
    r = _sigmoid(jnp.concatenate([p[:, :bw] for p in parts], axis=1) + gb[0:1])
    ig = _sigmoid(jnp.concatenate([p[:, bw:] for p in parts], axis=1) + gb[1:2])

    nlam = -lam_ref[...]
    softplus_nlam = jnp.maximum(nlam, 0.0) + jnp.log1p(jnp.exp(-jnp.abs(nlam)))
    log_a = (-LRU_C) * r * softplus_nlam
    a = jnp.exp(log_a)
    u = jnp.sqrt(1.0 - a * a) * (ig * xc)

    rows = lax.broadcasted_iota(I32, (tc, c), 0)
    d = 1
    while d < tc:
        keep = rows >= d
        a_prev = jnp.where(keep, pltpu.roll(a, d, 0), 1.0)
        u_prev = jnp.where(keep, pltpu.roll(u, d, 0), 0.0)
        u = a * u_prev + u
        a = a * a_prev
        d *= 2
    h = u + a * h_ref[...]
    h_ref[...] = h[tc - 1:tc, :]
    o_ref[...] = (h * jax.nn.gelu(gr_ref[...].astype(F32))).astype(o_ref.dtype)


def lru_branch(proj, conv_w, conv_b, gate_w, gate_b, lam, *, batch, seq, width, tc):
    nchunk = seq // tc
    return pl.pallas_call(
        functools.partial(_lru_kernel, tc=tc),
        grid=(batch, nchunk),
        in_specs=[
            pl.BlockSpec((tc, width), lambda b, i: (b * nchunk + i, 0)),
            pl.BlockSpec((tc, width), lambda b, i: (b * nchunk + i, 1)),
            pl.BlockSpec((CONV_WIDTH, width), lambda b, i: (0, 0)),
            pl.BlockSpec((1, width), lambda b, i: (0, 0)),
            pl.BlockSpec(gate_w.shape, lambda b, i: (0, 0, 0)),
            pl.BlockSpec((2, width), lambda b, i: (0, 0)),
            pl.BlockSpec((1, width), lambda b, i: (0, 0)),
        ],
        out_specs=pl.BlockSpec((tc, width), lambda b, i: (b * nchunk + i, 0)),
        out_shape=jax.ShapeDtypeStruct((batch * seq, width), BF16),
        scratch_shapes=[pltpu.VMEM((SUBLANES, width), F32), pltpu.VMEM((1, width), F32)],
        compiler_params=_params("arbitrary", "arbitrary"),
        name="lru",
    )(proj, proj, conv_w, conv_b.reshape(1, width), gate_w, gate_b, lam.reshape(1, width))


def _sba_blocks(qs, kbs, vbs, later, masks, runs, *, scale):
    n = len(qs)
    qb = qs[0].shape[0]
    zs = [lax.dot_general(q, kb, (((1,), (1,)), ((), ())), preferred_element_type=F32) * (scale * LOG2E)
          for q, kb in zip(qs, kbs)]
    softplus = [jnp.maximum(z, 0.0) + jnp.log(1.0 + jnp.exp2(-jnp.abs(z))) * LOG2E for z in zs]
    log_1m = [jnp.where(m, -s, 0.0) for m, s in zip(masks, softplus)]
    his = [x.astype(BF16) for x in log_1m]
    los = [(x - h.astype(F32)).astype(BF16) for x, h in zip(log_1m, his)]
    sums = jnp.dot(jnp.concatenate(his + los, axis=0), later, preferred_element_type=F32)
    outs = []
    for c in range(n):
        rest = sums[c * qb:(c + 1) * qb] + sums[(n + c) * qb:(n + c + 1) * qb]
        log_w = (zs[c] - softplus[c]) + rest
        if runs[c] is not None:
            log_w = log_w + runs[c]
        w = jnp.where(masks[c], jnp.exp2(log_w), 0.0)
        outs.append((w.astype(BF16), rest[:, 0:1] + log_1m[c][:, 0:1]))
    return [(jnp.dot(w, vb, preferred_element_type=F32), total) for (w, total), vb in zip(outs, vbs)]


def _sba_kernel(q_ref, k_ref, v_ref, o_ref, later_ref, acc_ref, run_ref, *, qb, nchain, wk, fk, scale):
    step_id = pl.program_id(2)
    dead_below = F32_EXP_ZERO * LOG2E

    @pl.when(step_id == 0)
    def _():
        r = lax.broadcasted_iota(I32, (wk, wk), 0)
        c = lax.broadcasted_iota(I32, (wk, wk), 1)
        later_ref[...] = jnp.where(r > c, 1.0, 0.0).astype(BF16)

    col_minus_row = (lax.broadcasted_iota(I32, (qb, wk), 1) - lax.broadcasted_iota(I32, (qb, wk), 0))

    def first_key(c):
        q0 = (step_id * nchain + c) * qb
        return q0, pl.multiple_of(jnp.maximum(q0 + qb - wk, 0), qb)

    lefts = []
    masks = []
    for c in range(nchain):
        q0, left = first_key(c)
        lefts.append(left)
        masks.append(col_minus_row < q0 - left)
    first = _sba_blocks([q_ref[c * qb:(c + 1) * qb, :] for c in range(nchain)],
                        [k_ref[pl.ds(left, wk), :] for left in lefts],
                        [v_ref[pl.ds(left, wk), :] for left in lefts],
                        later_ref[...], masks, [None] * nchain, scale=scale)
    worst = first[0][1]
    for c, (out, run) in enumerate(first):
        o_ref[c * qb:(c + 1) * qb, :] = out.astype(o_ref.dtype)
        worst = jnp.maximum(worst, run)

    @pl.when(jnp.max(worst) > dead_below)
    def _():
        for c, (out, run) in enumerate(first):
            acc_ref[c] = out
            run_ref[c] = run
        key_col = lax.broadcasted_iota(I32, (qb, fk), 1)

        def walk_left(c, carry):
            rows = pl.ds(pl.multiple_of(c * qb, qb), qb)

            def cond(state):
                left, live = state
                return jnp.logical_and(left > 0, live)

            def body(state):
                left, _ = state
                start = pl.multiple_of(jnp.maximum(left - fk, 0), qb)
                run = run_ref[c]
                (out, more), = _sba_blocks([q_ref[rows, :]], [k_ref[pl.ds(start, fk), :]],
                                           [v_ref[pl.ds(start, fk), :]], later_ref[0:fk, 0:fk],
                                           [key_col < left - start], [run], scale=scale)
                acc_ref[c] += out
                run_ref[c] = run + more
                return start, jnp.max(run + more) > dead_below

            lax.while_loop(cond, body, (first_key(c)[1], jnp.max(run_ref[c]) > dead_below))
            o_ref[rows, :] = acc_ref[c].astype(o_ref.dtype)
            return carry

        lax.fori_loop(0, nchain, walk_left, 0)


def stick_breaking(proj, *, batch, seq, heads, head_dim, q_col, k_col, v_col, qb, nchain, wk, fk):
    nstep = seq // (qb * nchain)
    rows = qb * nchain
    return pl.pallas_call(
        functools.partial(_sba_kernel, qb=qb, nchain=nchain, wk=wk, fk=fk, scale=head_dim ** -0.5),
        grid=(batch, heads, nstep),
        in_specs=[
            pl.BlockSpec((rows, head_dim), lambda b, h, i: (b * nstep + i, q_col + h)),
            pl.BlockSpec((seq, head_dim), lambda b, h, i: (b, k_col + h)),
            pl.BlockSpec((seq, head_dim), lambda b, h, i: (b, v_col + h)),
        ],
        out_specs=pl.BlockSpec((rows, head_dim), lambda b, h, i: (b * nstep + i, h)),
        out_shape=jax.ShapeDtypeStruct((batch * seq, heads * head_dim), BF16),
        scratch_shapes=[
            pltpu.VMEM((wk, wk), BF16),
            pltpu.VMEM((nchain, qb, head_dim), F32),
            pltpu.VMEM((nchain, qb, 1), F32),
        ],
        compiler_params=_params("arbitrary", "arbitrary", "arbitrary"),
        name="sba",
    )(proj, proj, proj)


def _out_router_kernel(*refs, n_in):
    y_refs = refs[:n_in]
    w_refs = refs[n_in:2 * n_in]
    x_ref, g_ref, rwh_ref, rwl_ref, rb_ref = refs[2 * n_in:2 * n_in + 5]
    xo_ref, hp_ref, route_ref, cnt_ref, carry_ref = refs[2 * n_in + 5:]

    @pl.when(pl.program_id(0) == 0)
    def _():
        carry_ref[...] = jnp.zeros_like(carry_ref)

    acc = x_ref[...]
    for y_ref, w_ref in zip(y_refs, w_refs):
        acc = acc + jnp.dot(y_ref[...], w_ref[...], preferred_element_type=F32)
    xo_ref[...] = acc

    tm, d = acc.shape
    ms = jnp.mean(acc * acc, axis=-1, keepdims=True)
    h = acc * lax.rsqrt(ms + EPS) * g_ref[...]
    hp_ref[...] = _pack_bf16_pair(h[:, :d // 2], h[:, d // 2:])

    h_hi = h.astype(BF16)
    h_lo = (h - h_hi.astype(F32)).astype(BF16)
    logits = (jnp.dot(h_hi, rwh_ref[...], preferred_element_type=F32)
              + jnp.dot(h_lo, rwh_ref[...], preferred_element_type=F32)
              + jnp.dot(h_hi, rwl_ref[...], preferred_element_type=F32)
              + rb_ref[...])

    lane = lax.broadcasted_iota(I32, (tm, LANES), 1)
    neg = jnp.float32(-jnp.inf)
    big = jnp.int32(LANES)

    def first_argmax(vals, mask):
        m = jnp.max(jnp.where(mask, vals, neg), axis=-1, keepdims=True)
        idx = jnp.min(jnp.where(jnp.logical_and(mask, vals == m), lane, big), axis=-1, keepdims=True)
        return m, idx

    is_group = lane < MOE_GROUPS
    g_max, g_idx = first_argmax(logits, is_group)
    g_w = 1.0 / jnp.sum(jnp.where(is_group, jnp.exp(logits - g_max), 0.0), axis=-1, keepdims=True)
    e_lo = MOE_GROUPS + g_idx * MOE_EXPERTS_PER_GROUP
    in_group = jnp.logical_and(lane >= e_lo, lane < e_lo + MOE_EXPERTS_PER_GROUP)
    l1, i1 = first_argmax(logits, in_group)
    l2, i2 = first_argmax(logits, jnp.logical_and(in_group, lane != i1))
    t = jnp.exp(l2 - l1)
    w1 = g_w / (1.0 + t)
    w2 = g_w * t / (1.0 + t)
    e1 = i1 - MOE_GROUPS
    e2 = i2 - MOE_GROUPS

    onehot = jnp.where(jnp.logical_or(lane == e1, lane == e2), 1.0, 0.0)
    rr = lax.broadcasted_iota(I32, (tm, tm), 0)
    cc = lax.broadcasted_iota(I32, (tm, tm), 1)
    lower = jnp.where(cc < rr, 1.0, 0.0).astype(BF16)
    before = jnp.dot(lower, onehot.astype(BF16), preferred_element_type=F32) + carry_ref[...]
    rank1 = jnp.sum(jnp.where(lane == e1, before, 0.0), axis=-1, keepdims=True)
    rank2 = jnp.sum(jnp.where(lane == e2, before, 0.0), axis=-1, keepdims=True)
    new_carry = before[tm - 1:tm, :] + onehot[tm - 1:tm, :]
    carry_ref[...] = new_carry
    cnt_ref[...] = jnp.broadcast_to(new_carry, cnt_ref.shape)

    route = jnp.where(lane == 0, e1.astype(F32), 0.0)
    route = jnp.where(lane == 1, e2.astype(F32), route)
    route = jnp.where(lane == 2, w1, route)
    route = jnp.where(lane == 3, w2, route)
    route = jnp.where(lane == 4, rank1, route)
    route = jnp.where(lane == 5, rank2, route)
    route_ref[...] = route


def out_router(ys, ws, x, g, rw_hi, rw_lo, rb, *, tm):
    t, d = x.shape
    n_in = len(ys)
    in_specs = ([pl.BlockSpec((tm, y.shape[1]), lambda i: (i, 0)) for y in ys]
                + [_resident(w.shape) for w in ws]
                + [pl.BlockSpec((tm, d), lambda i: (i, 0)),
                   _resident((1, d)), _resident(rw_hi.shape), _resident(rw_lo.shape), _resident((1, LANES))])
    return pl.pallas_call(
        functools.partial(_out_router_kernel, n_in=n_in),
        grid=(t // tm,),
        in_specs=in_specs,
        out_specs=[
            pl.BlockSpec((tm, d), lambda i: (i, 0)),
            pl.BlockSpec((tm, d // 2), lambda i: (i, 0)),
            pl.BlockSpec((tm, LANES), lambda i: (i, 0)),
            pl.BlockSpec((SUBLANES, LANES), lambda i: (0, 0)),
        ],
        out_shape=[
            jax.ShapeDtypeStruct((t, d), F32),
            jax.ShapeDtypeStruct((t, d // 2), U32),
            jax.ShapeDtypeStruct((t, LANES), F32),
            jax.ShapeDtypeStruct((SUBLANES, LANES), F32),
        ],
        scratch_shapes=[pltpu.VMEM((1, LANES), F32)],
        compiler_params=_params("arbitrary"),
        name="out_router",
    )(*ys, *ws, x, g.reshape(1, d), rw_hi, rw_lo, rb)


def _dispatch_kernel(dest_ref, h_ref, xs_ref, sem, *, tm, top_k):
    base = pl.program_id(0) * tm * top_k

    def row_copy(t, k):
        return pltpu.make_async_copy(h_ref.at[pl.ds(t, 1)],
                                     xs_ref.at[pl.ds(dest_ref[base + t * top_k + k], 1)], sem)

    def issue(t, carry):
        for k in range(top_k):
            row_copy(t, k).start()
        return carry

    lax.fori_loop(0, tm, issue, 0, unroll=DMA_ISSUE_UNROLL)

    for k in range(top_k):
        pltpu.make_async_copy(h_ref, xs_ref.at[pl.ds(0, tm)], sem).wait()


def dispatch(dest, h_packed, *, rows_out, tm, top_k):
    t, dw = h_packed.shape
    return pl.pallas_call(
        functools.partial(_dispatch_kernel, tm=tm, top_k=top_k),
        grid_spec=pltpu.PrefetchScalarGridSpec(
            num_scalar_prefetch=1,
            grid=(t // tm,),
            in_specs=[pl.BlockSpec((tm, dw), lambda i, dest: (i, 0))],
            out_specs=pl.BlockSpec(memory_space=pl.ANY),
            scratch_shapes=[pltpu.SemaphoreType.DMA(())],
        ),
        out_shape=jax.ShapeDtypeStruct((rows_out, dw), U32),
        compiler_params=pltpu.CompilerParams(dimension_semantics=("arbitrary",),
                                             vmem_limit_bytes=VMEM_LIMIT, has_side_effects=True),
        name="dispatch",
    )(dest, h_packed)


def _ffn_kernel(blk_e_ref, blk_valid_ref, nused_ref, x_ref, wg_ref, wu_ref, wd_ref, y_ref,
                wg_bf, wu_bf, wd_bf):
    j = pl.program_id(0)

    @pl.when(j < nused_ref[0])
    def _():
        @pl.when(jnp.logical_or(j == 0, blk_e_ref[j] != blk_e_ref[jnp.maximum(j - 1, 0)]))
        def _():
            wg_bf[...] = wg_ref[0, 0].astype(BF16)
            wu_bf[...] = wu_ref[0, 0].astype(BF16)
            wd_bf[...] = wd_ref[0, 0].astype(BF16)

        rows = lax.broadcasted_iota(I32, x_ref.shape, 0)
        packed = jnp.where(rows < blk_valid_ref[j], x_ref[...], jnp.uint32(0))
        a, b = _unpack_bf16_pair(packed)
        x = jnp.concatenate([a.astype(BF16), b.astype(BF16)], axis=1)
        gate = jnp.dot(x, wg_bf[...], preferred_element_type=F32)
        up = jnp.dot(x, wu_bf[...], preferred_element_type=F32)
        hid = (gate * _sigmoid(gate) * up).astype(BF16)
        y = jnp.dot(hid, wd_bf[...], preferred_element_type=F32)
        half = y.shape[1] // 2
        y_ref[...] = _pack_bf16_pair(y[:, :half], y[:, half:])


def expert_ffn(blk_e, blk_valid, nused, xs, w_gate, w_up, w_down, *, layer, tb):
    p, dw = xs.shape
    _, _, d, ff = w_gate.shape

    def row_map(j, be, bv, nu):
        return (jnp.minimum(j, nu[0] - 1), 0)

    def w_map(j, be, bv, nu):
        return (layer, be[j], 0, 0)

    return pl.pallas_call(
        _ffn_kernel,
        grid_spec=pltpu.PrefetchScalarGridSpec(
            num_scalar_prefetch=3,
            grid=(p // tb,),
            in_specs=[
                pl.BlockSpec((tb, dw), row_map),
                pl.BlockSpec((1, 1, d, ff), w_map),
                pl.BlockSpec((1, 1, d, ff), w_map),
                pl.BlockSpec((1, 1, ff, d), w_map),
            ],
            out_specs=pl.BlockSpec((tb, dw), row_map),
            scratch_shapes=[pltpu.VMEM((d, ff), BF16), pltpu.VMEM((d, ff), BF16), pltpu.VMEM((ff, d), BF16)],
        ),
        out_shape=jax.ShapeDtypeStruct((p, dw), U32),
        compiler_params=_params("arbitrary"),
        name="ffn",
    )(blk_e, blk_valid, nused, xs, w_gate, w_up, w_down)


def _combine_kernel(dest_ref, x_ref, route_ref, g_ref, ys_ref, o_ref, buf_ref, sem, *, tm, top_k, final_norm):
    i = pl.program_id(0)
    slot = i % 2

    def gather_tile(tile, slot):
        base = tile * tm * top_k

        def issue(t, carry):
            for k in range(top_k):
                pltpu.make_async_copy(ys_ref.at[pl.ds(dest_ref[base + t * top_k + k], 1)],
                                      buf_ref.at[slot, k, pl.ds(t, 1)], sem.at[slot]).start()
            return carry

        lax.fori_loop(0, tm, issue, 0, unroll=DMA_ISSUE_UNROLL)

    @pl.when(i == 0)
    def _():
        gather_tile(i, slot)

    @pl.when(i + 1 < pl.num_programs(0))
    def _():
        gather_tile(i + 1, 1 - slot)

    for k in range(top_k):
        pltpu.make_async_copy(ys_ref.at[pl.ds(0, tm)], buf_ref.at[slot, k], sem.at[slot]).wait()

    route = route_ref[...]
    out = x_ref[...]
    for k in range(top_k):
        a, b = _unpack_bf16_pair(buf_ref[slot, k])
        out = out + route[:, 2 + k:3 + k] * jnp.concatenate([a, b], axis=1)
    if final_norm:
        ms = jnp.mean(out * out, axis=-1, keepdims=True)
        out = out * lax.rsqrt(ms + EPS) * g_ref[...]
    o_ref[...] = out


def combine(dest, x, route, g, ys, *, tm, top_k, final_norm):
    t, d = x.shape
    dw = ys.shape[1]
    return pl.pallas_call(
        functools.partial(_combine_kernel, tm=tm, top_k=top_k, final_norm=final_norm),
        grid_spec=pltpu.PrefetchScalarGridSpec(
            num_scalar_prefetch=1,
            grid=(t // tm,),
            in_specs=[
                pl.BlockSpec((tm, d), lambda i, dest: (i, 0)),
                pl.BlockSpec((tm, LANES), lambda i, dest: (i, 0)),
                pl.BlockSpec((1, d), lambda i, dest: (0, 0)),
                pl.BlockSpec(memory_space=pl.ANY),
            ],
            out_specs=pl.BlockSpec((tm, d), lambda i, dest: (i, 0)),
            scratch_shapes=[pltpu.VMEM((2, top_k, tm, dw), U32), pltpu.SemaphoreType.DMA((2,))],
        ),
        out_shape=jax.ShapeDtypeStruct((t, d), F32),
        compiler_params=_params("arbitrary"),
        name="combine",
    )(dest, x, route, g.reshape(1, d), ys)


def moe_layer(x_new, h_packed, route, counts, w_gate, w_up, w_down, g_final, *, layer, tb, tm_rows, final_norm):
    t = x_new.shape[0]
    top_k = 2
    cnt = counts[0, :MOE_EXPERTS].astype(I32)
    padded = ((cnt + tb - 1) // tb) * tb
    pends = jnp.cumsum(padded)
    pstarts = pends - padded
    expert_ids = jnp.arange(MOE_EXPERTS, dtype=I32)
    experts = route[:, 0:top_k].astype(I32)
    ranks = route[:, 4:4 + top_k].astype(I32)
    start_of = jnp.sum(jnp.where(experts[..., None] == expert_ids, pstarts, 0), axis=-1)
    dest = (start_of + ranks).reshape(t * top_k)
    rows_out = t * top_k + MOE_EXPERTS * tb
    nb = rows_out // tb
    blk_start = jnp.arange(nb, dtype=I32) * tb
    blk_e = jnp.minimum(jnp.sum((blk_start[:, None] >= pends[None, :]).astype(I32), axis=1), MOE_EXPERTS - 1)
    blk_onehot = blk_e[:, None] == expert_ids
    blk_end = jnp.sum(jnp.where(blk_onehot, pstarts + cnt, 0), axis=1)
    blk_valid = jnp.clip(blk_end - blk_start, 0, tb).astype(I32)
    nused = (pends[-1:] // tb).astype(I32)

    xs = dispatch(dest, h_packed, rows_out=rows_out, tm=tm_rows, top_k=top_k)
    ys = expert_ffn(blk_e, blk_valid, nused, xs, w_gate, w_up, w_down, layer=layer, tb=tb)
    return combine(dest, x_new, route, g_final, ys, tm=tm_rows, top_k=top_k, final_norm=final_norm)


def _rope_kernel(pos_ref, inv_ref, cos_ref, sin_ref):
    ang = pos_ref[...].astype(F32) * inv_ref[...]
    cos_ref[...] = jnp.cos(ang)
    sin_ref[...] = jnp.sin(ang)


def rope_table(positions, half, *, tm):
    t = positions.size
    inv_freq = (ROPE_BASE ** (-jnp.arange(half, dtype=F32) / half)).reshape(1, half)
    return pl.pallas_call(
        _rope_kernel,
        grid=(t // tm,),
        in_specs=[pl.BlockSpec((tm, 1), lambda i: (i, 0)), pl.BlockSpec((1, half), lambda i: (0, 0))],
        out_specs=[pl.BlockSpec((tm, half), lambda i: (i, 0))] * 2,
        out_shape=[jax.ShapeDtypeStruct((t, half), F32)] * 2,
        compiler_params=_params("arbitrary"),
        name="rope_table",
    )(positions.reshape(t, 1), inv_freq)


def _retention_kernel(lg_ref, q_ref, k_ref, v_ref, g_ref, cos_ref, sin_ref, gain_ref, o_ref,
                      state_ref, decay_ref, qd_ref, kd_ref, *, c):
    dk = q_ref.shape[1]
    dv = v_ref.shape[1]
    half = dk // 2
    lg = lg_ref[pl.program_id(1)]

    @pl.when(pl.program_id(2) == 0)
    def _():
        state_ref[...] = jnp.zeros_like(state_ref)
        ri = lax.broadcasted_iota(I32, (c, c), 0)
        ci = lax.broadcasted_iota(I32, (c, c), 1)
        diff = (ri - ci).astype(F32)
        decay_ref[...] = jnp.where(diff >= 0.0, jnp.exp(jnp.maximum(diff, 0.0) * lg), 0.0)
        pos = lax.broadcasted_iota(I32, (c, dk), 0).astype(F32)
        qd_ref[...] = jnp.exp((pos + 1.0) * lg)
        kd_ref[...] = jnp.exp((c - 1.0 - pos) * lg)

    cos = cos_ref[...]
    sin = sin_ref[...]

    def rotate(x):
        x1, x2 = x[:, :half], x[:, half:]
        return jnp.concatenate([x1 * cos - x2 * sin, x2 * cos + x1 * sin], axis=1)

    qr = rotate(q_ref[...].astype(F32))
    kr = rotate(k_ref[...].astype(F32)) * (dk ** -0.5)
    v = v_ref[...]

    s = lax.dot_general(qr.astype(BF16), kr.astype(BF16), (((1,), (1,)), ((), ())),
                        preferred_element_type=F32) * decay_ref[...]
    state = state_ref[...]
    y = (jnp.dot(s.astype(BF16), v, preferred_element_type=F32)
         + jnp.dot((qr * qd_ref[...]).astype(BF16), state.astype(BF16), preferred_element_type=F32))
    chunk_decay = jnp.exp(jnp.zeros((1, dv), F32) + c * lg)
    state_ref[...] = state * chunk_decay + lax.dot_general(
        (kr * kd_ref[...]).astype(BF16), v, (((0,), (0,)), ((), ())), preferred_element_type=F32)

    mu = jnp.mean(y, axis=-1, keepdims=True)
    yc = y - mu
    var = jnp.mean(yc * yc, axis=-1, keepdims=True)
    yn = yc * lax.rsqrt(var + EPS) * gain_ref[0]
    g = g_ref[...].astype(F32)
    o_ref[...] = (g * _sigmoid(g) * yn).astype(o_ref.dtype)


def retention(proj, cos, sin, gain, *, batch, seq, heads, dk, dv, c):
    nchunk = seq // c
    log_g = jnp.log1p(-jnp.power(2.0, -5.0 - jnp.arange(heads, dtype=F32)))
    k_col = heads
    v_col = 2 * heads * dk // dv
    g_col = v_col + heads
    half = dk // 2

    def rows(b, h, n, lg):
        return b * nchunk + n

    return pl.pallas_call(
        functools.partial(_retention_kernel, c=c),
        grid_spec=pltpu.PrefetchScalarGridSpec(
            num_scalar_prefetch=1,
            grid=(batch, heads, nchunk),
            in_specs=[
                pl.BlockSpec((c, dk), lambda b, h, n, lg: (rows(b, h, n, lg), h)),
                pl.BlockSpec((c, dk), lambda b, h, n, lg: (rows(b, h, n, lg), k_col + h)),
                pl.BlockSpec((c, dv), lambda b, h, n, lg: (rows(b, h, n, lg), v_col + h)),
                pl.BlockSpec((c, dv), lambda b, h, n, lg: (rows(b, h, n, lg), g_col + h)),
                pl.BlockSpec((c, half), lambda b, h, n, lg: (rows(b, h, n, lg), 0)),
                pl.BlockSpec((c, half), lambda b, h, n, lg: (rows(b, h, n, lg), 0)),
                pl.BlockSpec((1, 1, dv), lambda b, h, n, lg: (h, 0, 0)),
            ],
            out_specs=pl.BlockSpec((c, dv), lambda b, h, n, lg: (rows(b, h, n, lg), h)),
            scratch_shapes=[
                pltpu.VMEM((dk, dv), F32),
                pltpu.VMEM((c, c), F32),
                pltpu.VMEM((c, dk), F32),
                pltpu.VMEM((c, dk), F32),
            ],
        ),
        out_shape=jax.ShapeDtypeStruct((batch * seq, heads * dv), BF16),
        compiler_params=_params("arbitrary", "arbitrary", "arbitrary"),
        name="retention",
    )(log_g, proj, proj, proj, proj, cos, sin, gain.reshape(heads, 1, dv))


def _router_params(wg, bg, we, be):
    d = wg.shape[0]
    n = MOE_GROUPS + MOE_EXPERTS
    w = jnp.concatenate([wg, we, jnp.zeros((d, LANES - n), F32)], axis=1)
    b = jnp.concatenate([bg, be, jnp.zeros((LANES - n,), F32)]).reshape(1, LANES)
    w_hi = w.astype(BF16)
    w_lo = (w - w_hi.astype(F32)).astype(BF16)
    return w_hi, w_lo, b


def _pick(total, want):
    t = min(want, total)
    while total % t:
        t //= 2
    return t


def kernel(x, positions, ev_norm, ev_w_in, ev_conv_w, ev_conv_b, ev_gate_a_w, ev_gate_a_b, ev_gate_x_w, ev_gate_x_b, ev_lru_lambda, ev_w_out, od_norm, od_w_in, od_gn_gain, od_w_out, ffn_norm, router_group_w, router_group_b, router_expert_w, router_expert_b, expert_w_gate, expert_w_up, expert_w_down, final_norm):
    batch, seq, d = x.shape
    t = batch * seq
    depth = ffn_norm.shape[0]
    lru_w = ev_conv_w.shape[2]
    sb_w = (ev_w_in.shape[2] - 2 * lru_w) // 3
    head_dim = sb_w // SB_HEADS
    dv = od_gn_gain.shape[2]
    dk = (od_w_in.shape[2] - 2 * RET_HEADS * dv) // (2 * RET_HEADS)

    tm_proj = _pick(t, 1024)
    tm_out = _pick(t, 512)
    tm_rows = _pick(t, 512)
    tb = 256

    xf = x.reshape(t, d)
    for layer in range(depth):
        i = layer // 2
        if layer % 2 == 0:
            proj = norm_matmul(xf, ev_norm[i], ev_w_in[i].astype(BF16), tm=tm_proj, tn=_pick(ev_w_in.shape[2], 1024))
            gate_w = jnp.concatenate([ev_gate_a_w[i], ev_gate_x_w[i]], axis=2).astype(BF16)
            gate_b = jnp.stack([ev_gate_a_b[i].reshape(lru_w), ev_gate_x_b[i].reshape(lru_w)])
            y_a = lru_branch(proj, ev_conv_w[i], ev_conv_b[i], gate_w, gate_b, ev_lru_lambda[i],
                             batch=batch, seq=seq, width=lru_w, tc=_pick(seq, 256))
            col0 = 2 * lru_w // head_dim
            y_b = stick_breaking(proj, batch=batch, seq=seq, heads=SB_HEADS, head_dim=head_dim,
                                 q_col=col0, k_col=col0 + SB_HEADS, v_col=col0 + 2 * SB_HEADS,
                                 qb=SBA_QUERY_BLOCK, nchain=SBA_CHAINS, wk=SBA_FIRST_KEYS, fk=SBA_MORE_KEYS)
            w_out = ev_w_out[i].astype(BF16)
            ys, ws = [y_a, y_b], [w_out[:lru_w], w_out[lru_w:]]
        else:
            proj = norm_matmul(xf, od_norm[i], od_w_in[i].astype(BF16), tm=tm_proj, tn=_pick(od_w_in.shape[2], 1024))
            cos, sin = rope_table(positions, dk // 2, tm=_pick(t, 1024))
            y = retention(proj, cos, sin, od_gn_gain[i], batch=batch, seq=seq, heads=RET_HEADS,
                          dk=dk, dv=dv, c=_pick(seq, 256))
            ys, ws = [y], [od_w_out[i].astype(BF16)]

        rw_hi, rw_lo, rb = _router_params(router_group_w[layer], router_group_b[layer],
                                          router_expert_w[layer], router_expert_b[layer])
        x_new, h_packed, route, counts = out_router(ys, ws, xf, ffn_norm[layer], rw_hi, rw_lo, rb, tm=tm_out)
        last = layer == depth - 1
        xf = moe_layer(x_new, h_packed, route, counts, expert_w_gate, expert_w_up, expert_w_down, final_norm,
                       layer=layer, tb=tb, tm_rows=tm_rows, final_norm=last)
    return xf.reshape(batch, seq, d)
```

```python
import functools

import jax
import jax.numpy as jnp
from jax import lax
from jax.experimental import pallas as pl
from jax.experimental.pallas import tpu as pltpu

F32 = jnp.float32
BF16 = jnp.bfloat16
U32 = jnp.uint32
I32 = jnp.int32

LANES = 128
SUBLANES = 8
VMEM_BYTES_V7X = 64 * 1024 * 1024
VMEM_LIMIT = VMEM_BYTES_V7X * 7 // 8

EPS = 1e-6
LRU_BLOCKS = 8
CONV_WIDTH = 4
LRU_C = 8.0
SB_HEADS = 8
RET_HEADS = 8
ROPE_BASE = 10000.0
MOE_GROUPS = 4
MOE_EXPERTS_PER_GROUP = 8
MOE_EXPERTS = MOE_GROUPS * MOE_EXPERTS_PER_GROUP

SBA_QUERY_BLOCK = 64
SBA_CHAINS = 16
SBA_FIRST_KEYS = 256
SBA_MORE_KEYS = 128
LOG2E = 1.4426950408889634

RET_HEADS_PER_STEP = 4

DMA_ISSUE_UNROLL = 8

F32_EXP_ZERO = -104.0

HI16 = 0xFFFF0000


def _params(*sem):
    return pltpu.CompilerParams(dimension_semantics=sem, vmem_limit_bytes=VMEM_LIMIT)


def _resident(shape):
    return pl.BlockSpec(shape, lambda *_: (0,) * len(shape), pipeline_mode=pl.Buffered(1))


def _pack_bf16_pair(a, b):
    ua = lax.bitcast_convert_type(a.astype(BF16).astype(F32), U32)
    ub = lax.bitcast_convert_type(b.astype(BF16).astype(F32), U32)
    return (ua >> 16) | (ub & jnp.uint32(HI16))


def _unpack_bf16_pair(w):
    a = lax.bitcast_convert_type(w << 16, F32)
    b = lax.bitcast_convert_type(w & jnp.uint32(HI16), F32)
    return a, b


def _norm_matmul_kernel(x_ref, g_ref, w_ref, o_ref, h_ref):
    @pl.when(pl.program_id(1) == 0)
    def _():
        x = x_ref[...]
        ms = jnp.mean(x * x, axis=-1, keepdims=True)
        h_ref[...] = (x * lax.rsqrt(ms + EPS) * g_ref[...]).astype(h_ref.dtype)

    o_ref[...] = jnp.dot(h_ref[...], w_ref[...], preferred_element_type=F32).astype(o_ref.dtype)


def norm_matmul(x, g, w, *, tm, tn):
    t, d = x.shape
    n = w.shape[1]
    return pl.pallas_call(
        _norm_matmul_kernel,
        grid=(t // tm, n // tn),
        in_specs=[
            pl.BlockSpec((tm, d), lambda i, j: (i, 0)),
            pl.BlockSpec((1, d), lambda i, j: (0, 0)),
            pl.BlockSpec((d, tn), lambda i, j: (0, j)),
        ],
        out_specs=pl.BlockSpec((tm, tn), lambda i, j: (i, j)),
        out_shape=jax.ShapeDtypeStruct((t, n), BF16),
        scratch_shapes=[pltpu.VMEM((tm, d), BF16)],
        compiler_params=_params("arbitrary", "arbitrary"),
        name="norm_matmul",
    )(x, g.reshape(1, d), w)


def _sigmoid(x):
    return 1.0 / (1.0 + jnp.exp(-x))


def _lru_kernel(xr_ref, gr_ref, cw_ref, cb_ref, gw_ref, gb_ref, lam_ref, o_ref, tail_ref, h_ref, *, tc):
    @pl.when(pl.program_id(1) == 0)
    def _():
        tail_ref[...] = jnp.zeros_like(tail_ref)
        h_ref[...] = jnp.zeros_like(h_ref)

    c = xr_ref.shape[1]
    bw = c // LRU_BLOCKS
    x = xr_ref[...].astype(F32)
    xe = jnp.concatenate([tail_ref[...], x], axis=0)
    cw = cw_ref[...]
    xc = cb_ref[...] + cw[3:4] * x
    for k in range(CONV_WIDTH - 1):
        off = SUBLANES - (CONV_WIDTH - 1 - k)
        xc = xc + cw[k:k + 1] * xe[off:off + tc]
    tail_ref[...] = x[tc - SUBLANES:, :]

    xcb = xc.astype(BF16)
    parts = [jnp.dot(xcb[:, g * bw:(g + 1) * bw], gw_ref[g], preferred_element_type=F32)
             for g in range(LRU_BLOCKS)]
    gb = gb_ref[...]
    r = _sigmoid(jnp.concatenate([p[:, :bw] for p in parts], axis=1) + gb[0:1])
    ig = _sigmoid(jnp.concatenate([p[:, bw:] for p in parts], axis=1) + gb[1:2])

    nlam = -lam_ref[...]
    softplus_nlam = jnp.maximum(nlam, 0.0) + jnp.log1p(jnp.exp(-jnp.abs(nlam)))
    log_a = (-LRU_C) * r * softplus_nlam
    a = jnp.exp(log_a)
    u = jnp.sqrt(1.0 - a * a) * (ig * xc)

    groups = tc // SUBLANES
    a = a.reshape(groups, SUBLANES, c)
    u = u.reshape(groups, SUBLANES, c)
    row_in_group = lax.broadcasted_iota(I32, (groups, SUBLANES, c), 1)
    d = 1
    while d < SUBLANES:
        keep = row_in_group >= d
        a_prev = jnp.where(keep, pltpu.roll(a, d, 1), 1.0)
        u_prev = jnp.where(keep, pltpu.roll(u, d, 1), 0.0)
        u = a * u_prev + u
        a = a * a_prev
        d *= 2
    carry = h_ref[...]
    hs = []
    for g in range(groups):
        hg = u[g] + a[g] * carry
        carry = hg[SUBLANES - 1:SUBLANES, :]
        hs.append(hg)
    h = jnp.concatenate(hs, axis=0)
    h_ref[...] = carry
    o_ref[...] = (h * jax.nn.gelu(gr_ref[...].astype(F32))).astype(o_ref.dtype)


def lru_branch(proj, conv_w, conv_b, gate_w, gate_b, lam, *, batch, seq, width, tc):
    nchunk = seq // tc
    return pl.pallas_call(
        functools.partial(_lru_kernel, tc=tc),
        grid=(batch, nchunk),
        in_specs=[
            pl.BlockSpec((tc, width), lambda b, i: (b * nchunk + i, 0)),
            pl.BlockSpec((tc, width), lambda b, i: (b * nchunk + i, 1)),
            pl.BlockSpec((CONV_WIDTH, width), lambda b, i: (0, 0)),
            pl.BlockSpec((1, width), lambda b, i: (0, 0)),
            pl.BlockSpec(gate_w.shape, lambda b, i: (0, 0, 0)),
            pl.BlockSpec((2, width), lambda b, i: (0, 0)),
            pl.BlockSpec((1, width), lambda b, i: (0, 0)),
        ],
        out_specs=pl.BlockSpec((tc, width), lambda b, i: (b * nchunk + i, 0)),
        out_shape=jax.ShapeDtypeStruct((batch * seq, width), BF16),
        scratch_shapes=[pltpu.VMEM((SUBLANES, width), F32), pltpu.VMEM((1, width), F32)],
        compiler_params=_params("arbitrary", "arbitrary"),
        name="lru",
    )(proj, proj, conv_w, conv_b.reshape(1, width), gate_w, gate_b, lam.reshape(1, width))


def _sba_blocks(qs, kbs, vbs, later, masks, runs, *, scale):
    n = len(qs)
    qb = qs[0].shape[0]
    zs = [lax.dot_general(q, kb, (((1,), (1,)), ((), ())), preferred_element_type=F32) * (scale * LOG2E)
          for q, kb in zip(qs, kbs)]
    softplus = [jnp.maximum(z, 0.0) + jnp.log(1.0 + jnp.exp2(-jnp.abs(z))) * LOG2E for z in zs]
    log_1m = [jnp.where(m, -s, 0.0) for m, s in zip(masks, softplus)]
    his = [x.astype(BF16) for x in log_1m]
    los = [(x - h.astype(F32)).astype(BF16) for x, h in zip(log_1m, his)]
    sums = jnp.dot(jnp.concatenate(his + los, axis=0), later, preferred_element_type=F32)
    outs = []
    for c in range(n):
        rest = sums[c * qb:(c + 1) * qb] + sums[(n + c) * qb:(n + c + 1) * qb]
        log_w = (zs[c] - softplus[c]) + rest
        if runs[c] is not None:
            log_w = log_w + runs[c]
        w = jnp.where(masks[c], jnp.exp2(log_w), 0.0)
        outs.append((w.astype(BF16), rest[:, 0:1] + log_1m[c][:, 0:1]))
    return [(jnp.dot(w, vb, preferred_element_type=F32), total) for (w, total), vb in zip(outs, vbs)]


def _sba_kernel(q_ref, k_ref, v_ref, o_ref, later_ref, acc_ref, run_ref, *, qb, nchain, wk, fk, scale):
    step_id = pl.program_id(2)
    dead_below = F32_EXP_ZERO * LOG2E

    @pl.when(step_id == 0)
    def _():
        r = lax.broadcasted_iota(I32, (wk, wk), 0)
        c = lax.broadcasted_iota(I32, (wk, wk), 1)
        later_ref[...] = jnp.where(r > c, 1.0, 0.0).astype(BF16)

    col_minus_row = (lax.broadcasted_iota(I32, (qb, wk), 1) - lax.broadcasted_iota(I32, (qb, wk), 0))

    def first_key(c):
        q0 = (step_id * nchain + c) * qb
        return q0, pl.multiple_of(jnp.maximum(q0 + qb - wk, 0), qb)

    lefts = []
    masks = []
    for c in range(nchain):
        q0, left = first_key(c)
        lefts.append(left)
        masks.append(col_minus_row < q0 - left)
    first = _sba_blocks([q_ref[c * qb:(c + 1) * qb, :] for c in range(nchain)],
                        [k_ref[pl.ds(left, wk), :] for left in lefts],
                        [v_ref[pl.ds(left, wk), :] for left in lefts],
                        later_ref[...], masks, [None] * nchain, scale=scale)
    worst = first[0][1]
    for c, (out, run) in enumerate(first):
        o_ref[c * qb:(c + 1) * qb, :] = out.astype(o_ref.dtype)
        worst = jnp.maximum(worst, run)

    @pl.when(jnp.max(worst) > dead_below)
    def _():
        for c, (out, run) in enumerate(first):
            acc_ref[c] = out
            run_ref[c] = run
        key_col = lax.broadcasted_iota(I32, (qb, fk), 1)

        def walk_left(c, carry):
            rows = pl.ds(pl.multiple_of(c * qb, qb), qb)

            def cond(state):
                left, live = state
                return jnp.logical_and(left > 0, live)

            def body(state):
                left, _ = state
                start = pl.multiple_of(jnp.maximum(left - fk, 0), qb)
                run = run_ref[c]
                (out, more), = _sba_blocks([q_ref[rows, :]], [k_ref[pl.ds(start, fk), :]],
                                           [v_ref[pl.ds(start, fk), :]], later_ref[0:fk, 0:fk],
                                           [key_col < left - start], [run], scale=scale)
                acc_ref[c] += out
                run_ref[c] = run + more
                return start, jnp.max(run + more) > dead_below

            lax.while_loop(cond, body, (first_key(c)[1], jnp.max(run_ref[c]) > dead_below))
            o_ref[rows, :] = acc_ref[c].astype(o_ref.dtype)
            return carry

        lax.fori_loop(0, nchain, walk_left, 0)


def stick_breaking(proj, *, batch, seq, heads, head_dim, q_col, k_col, v_col, qb, nchain, wk, fk):
    nstep = seq // (qb * nchain)
    rows = qb * nchain
    return pl.pallas_call(
        functools.partial(_sba_kernel, qb=qb, nchain=nchain, wk=wk, fk=fk, scale=head_dim ** -0.5),
        grid=(batch, heads, nstep),
        in_specs=[
            pl.BlockSpec((rows, head_dim), lambda b, h, i: (b * nstep + i, q_col + h)),
            pl.BlockSpec((seq, head_dim), lambda b, h, i: (b, k_col + h)),
            pl.BlockSpec((seq, head_dim), lambda b, h, i: (b, v_col + h)),
        ],
        out_specs=pl.BlockSpec((rows, head_dim), lambda b, h, i: (b * nstep + i, h)),
        out_shape=jax.ShapeDtypeStruct((batch * seq, heads * head_dim), BF16),
        scratch_shapes=[
            pltpu.VMEM((wk, wk), BF16),
            pltpu.VMEM((nchain, qb, head_dim), F32),
            pltpu.VMEM((nchain, qb, 1), F32),
        ],
        compiler_params=_params("arbitrary", "arbitrary", "arbitrary"),
        name="sba",
    )(proj, proj, proj)


def _out_router_kernel(*refs, n_in):
    y_refs = refs[:n_in]
    w_refs = refs[n_in:2 * n_in]
    x_ref, g_ref, rw_ref, rb_ref = refs[2 * n_in:2 * n_in + 4]
    xo_ref, hp_ref, route_ref, cnt_ref, carry_ref = refs[2 * n_in + 4:]

    @pl.when(pl.program_id(0) == 0)
    def _():
        carry_ref[...] = jnp.zeros_like(carry_ref)

    acc = x_ref[...]
    for y_ref, w_ref in zip(y_refs, w_refs):
        acc = acc + jnp.dot(y_ref[...], w_ref[...], preferred_element_type=F32)
    xo_ref[...] = acc

    tm, d = acc.shape
    ms = jnp.mean(acc * acc, axis=-1, keepdims=True)
    h = acc * lax.rsqrt(ms + EPS) * g_ref[...]
    hp_ref[...] = _pack_bf16_pair(h[:, :d // 2], h[:, d // 2:])

    h_hi = h.astype(BF16)
    h_lo = (h - h_hi.astype(F32)).astype(BF16)
    logits = (jnp.dot(h_hi, rw_ref[:, :LANES], preferred_element_type=F32)
              + jnp.dot(h_lo, rw_ref[:, :LANES], preferred_element_type=F32)
              + jnp.dot(h_hi, rw_ref[:, LANES:], preferred_element_type=F32)
              + rb_ref[...])

    lane = lax.broadcasted_iota(I32, (tm, LANES), 1)
    neg = jnp.float32(-jnp.inf)
    big = jnp.int32(LANES)

    def first_argmax(vals, mask):
        m = jnp.max(jnp.where(mask, vals, neg), axis=-1, keepdims=True)
        idx = jnp.min(jnp.where(jnp.logical_and(mask, vals == m), lane, big), axis=-1, keepdims=True)
        return m, idx

    is_group = lane < MOE_GROUPS
    g_max, g_idx = first_argmax(logits, is_group)
    g_w = 1.0 / jnp.sum(jnp.where(is_group, jnp.exp(logits - g_max), 0.0), axis=-1, keepdims=True)
    e_lo = MOE_GROUPS + g_idx * MOE_EXPERTS_PER_GROUP
    in_group = jnp.logical_and(lane >= e_lo, lane < e_lo + MOE_EXPERTS_PER_GROUP)
    l1, i1 = first_argmax(logits, in_group)
    l2, i2 = first_argmax(logits, jnp.logical_and(in_group, lane != i1))
    t = jnp.exp(l2 - l1)
    w1 = g_w / (1.0 + t)
    w2 = g_w * t / (1.0 + t)
    e1 = i1 - MOE_GROUPS
    e2 = i2 - MOE_GROUPS

    onehot = jnp.where(jnp.logical_or(lane == e1, lane == e2), 1.0, 0.0)
    lower = jnp.where(lax.broadcasted_iota(I32, (tm, tm), 1) < lax.broadcasted_iota(I32, (tm, tm), 0),
                      1.0, 0.0).astype(BF16)
    before = jnp.dot(lower, onehot.astype(BF16), preferred_element_type=F32) + carry_ref[...]
    rank1 = jnp.sum(jnp.where(lane == e1, before, 0.0), axis=-1, keepdims=True)
    rank2 = jnp.sum(jnp.where(lane == e2, before, 0.0), axis=-1, keepdims=True)
    new_carry = before[tm - 1:tm, :] + onehot[tm - 1:tm, :]
    carry_ref[...] = new_carry
    cnt_ref[...] = jnp.broadcast_to(new_carry, cnt_ref.shape)

    route = jnp.where(lane == 0, e1.astype(F32), 0.0)
    route = jnp.where(lane == 1, e2.astype(F32), route)
    route = jnp.where(lane == 2, w1, route)
    route = jnp.where(lane == 3, w2, route)
    route = jnp.where(lane == 4, rank1, route)
    route = jnp.where(lane == 5, rank2, route)
    route_ref[...] = route


def out_router(ys, ws, x, g, rw, rb, *, tm):
    t, d = x.shape
    n_in = len(ys)
    in_specs = ([pl.BlockSpec((tm, y.shape[1]), lambda i: (i, 0)) for y in ys]
                + [_resident(w.shape) for w in ws]
                + [pl.BlockSpec((tm, d), lambda i: (i, 0)),
                   _resident((1, d)), _resident(rw.shape), _resident((1, LANES))])
    return pl.pallas_call(
        functools.partial(_out_router_kernel, n_in=n_in),
        grid=(t // tm,),
        in_specs=in_specs,
        out_specs=[
            pl.BlockSpec((tm, d), lambda i: (i, 0)),
            pl.BlockSpec((tm, d // 2), lambda i: (i, 0)),
            pl.BlockSpec((tm, LANES), lambda i: (i, 0)),
            pl.BlockSpec((SUBLANES, LANES), lambda i: (0, 0)),
        ],
        out_shape=[
            jax.ShapeDtypeStruct((t, d), F32),
            jax.ShapeDtypeStruct((t, d // 2), U32),
            jax.ShapeDtypeStruct((t, LANES), F32),
            jax.ShapeDtypeStruct((SUBLANES, LANES), F32),
        ],
        scratch_shapes=[pltpu.VMEM((1, LANES), F32)],
        compiler_params=_params("arbitrary"),
        name="out_router",
    )(*ys, *ws, x, g.reshape(1, d), rw, rb)


def _dispatch_kernel(dest_ref, h_ref, xs_ref, sem, *, tm, top_k):
    base = pl.program_id(0) * tm * top_k

    def row_copy(t, k):
        return pltpu.make_async_copy(h_ref.at[pl.ds(t, 1)],
                                     xs_ref.at[pl.ds(dest_ref[base + t * top_k + k], 1)], sem)

    def issue(t, carry):
        for k in range(top_k):
            row_copy(t, k).start()
        return carry

    lax.fori_loop(0, tm, issue, 0, unroll=DMA_ISSUE_UNROLL)

    for k in range(top_k):
        pltpu.make_async_copy(h_ref, xs_ref.at[pl.ds(0, tm)], sem).wait()


def dispatch(dest, h_packed, *, rows_out, tm, top_k):
    t, dw = h_packed.shape
    return pl.pallas_call(
        functools.partial(_dispatch_kernel, tm=tm, top_k=top_k),
        grid_spec=pltpu.PrefetchScalarGridSpec(
            num_scalar_prefetch=1,
            grid=(t // tm,),
            in_specs=[pl.BlockSpec((tm, dw), lambda i, dest: (i, 0))],
            out_specs=pl.BlockSpec(memory_space=pl.ANY),
            scratch_shapes=[pltpu.SemaphoreType.DMA(())],
        ),
        out_shape=jax.ShapeDtypeStruct((rows_out, dw), U32),
        compiler_params=pltpu.CompilerParams(dimension_semantics=("arbitrary",),
                                             vmem_limit_bytes=VMEM_LIMIT, has_side_effects=True),
        name="dispatch",
    )(dest, h_packed)


def _ffn_kernel(blk_e_ref, blk_valid_ref, nused_ref, x_ref, wg_ref, wu_ref, wd_ref, y_ref,
                wg_bf, wu_bf, wd_bf):
    j = pl.program_id(0)

    @pl.when(j < nused_ref[0])
    def _():
        @pl.when(jnp.logical_or(j == 0, blk_e_ref[j] != blk_e_ref[jnp.maximum(j - 1, 0)]))
        def _():
            wg_bf[...] = wg_ref[0, 0].astype(BF16)
            wu_bf[...] = wu_ref[0, 0].astype(BF16)
            wd_bf[...] = wd_ref[0, 0].astype(BF16)

        rows = lax.broadcasted_iota(I32, x_ref.shape, 0)
        packed = jnp.where(rows < blk_valid_ref[j], x_ref[...], jnp.uint32(0))
        a, b = _unpack_bf16_pair(packed)
        x = jnp.concatenate([a.astype(BF16), b.astype(BF16)], axis=1)
        gate = jnp.dot(x, wg_bf[...], preferred_element_type=F32)
        up = jnp.dot(x, wu_bf[...], preferred_element_type=F32)
        hid = (gate * _sigmoid(gate) * up).astype(BF16)
        y = jnp.dot(hid, wd_bf[...], preferred_element_type=F32)
        half = y.shape[1] // 2
        y_ref[...] = _pack_bf16_pair(y[:, :half], y[:, half:])


def expert_ffn(blk_e, blk_valid, nused, xs, w_gate, w_up, w_down, *, layer, tb):
    p, dw = xs.shape
    _, _, d, ff = w_gate.shape

    def row_map(j, be, bv, nu):
        return (jnp.minimum(j, nu[0] - 1), 0)

    def w_map(j, be, bv, nu):
        return (layer, be[j], 0, 0)

    return pl.pallas_call(
        _ffn_kernel,
        grid_spec=pltpu.PrefetchScalarGridSpec(
            num_scalar_prefetch=3,
            grid=(p // tb,),
            in_specs=[
                pl.BlockSpec((tb, dw), row_map),
                pl.BlockSpec((1, 1, d, ff), w_map),
                pl.BlockSpec((1, 1, d, ff), w_map),
                pl.BlockSpec((1, 1, ff, d), w_map),
            ],
            out_specs=pl.BlockSpec((tb, dw), row_map),
            scratch_shapes=[pltpu.VMEM((d, ff), BF16), pltpu.VMEM((d, ff), BF16), pltpu.VMEM((ff, d), BF16)],
        ),
        out_shape=jax.ShapeDtypeStruct((p, dw), U32),
        compiler_params=_params("arbitrary"),
        name="ffn",
    )(blk_e, blk_valid, nused, xs, w_gate, w_up, w_down)


def _combine_kernel(dest_ref, x_ref, route_ref, g_ref, ys_ref, o_ref, buf_ref, sem, *, tm, top_k, final_norm):
    i = pl.program_id(0)
    slot = i % 2

    def gather_tile(tile, slot):
        base = tile * tm * top_k

        def issue(t, carry):
            for k in range(top_k):
                pltpu.make_async_copy(ys_ref.at[pl.ds(dest_ref[base + t * top_k + k], 1)],
                                      buf_ref.at[slot, k, pl.ds(t, 1)], sem.at[slot]).start()
            return carry

        lax.fori_loop(0, tm, issue, 0, unroll=DMA_ISSUE_UNROLL)

    @pl.when(i == 0)
    def _():
        gather_tile(i, slot)

    @pl.when(i + 1 < pl.num_programs(0))
    def _():
        gather_tile(i + 1, 1 - slot)

    for k in range(top_k):
        pltpu.make_async_copy(ys_ref.at[pl.ds(0, tm)], buf_ref.at[slot, k], sem.at[slot]).wait()

    route = route_ref[...]
    out = x_ref[...]
    for k in range(top_k):
        a, b = _unpack_bf16_pair(buf_ref[slot, k])
        out = out + route[:, 2 + k:3 + k] * jnp.concatenate([a, b], axis=1)
    if final_norm:
        ms = jnp.mean(out * out, axis=-1, keepdims=True)
        out = out * lax.rsqrt(ms + EPS) * g_ref[...]
    o_ref[...] = out


def combine(dest, x, route, g, ys, *, tm, top_k, final_norm):
    t, d = x.shape
    dw = ys.shape[1]
    return pl.pallas_call(
        functools.partial(_combine_kernel, tm=tm, top_k=top_k, final_norm=final_norm),
        grid_spec=pltpu.PrefetchScalarGridSpec(
            num_scalar_prefetch=1,
            grid=(t // tm,),
            in_specs=[
                pl.BlockSpec((tm, d), lambda i, dest: (i, 0)),
                pl.BlockSpec((tm, LANES), lambda i, dest: (i, 0)),
                pl.BlockSpec((1, d), lambda i, dest: (0, 0)),
                pl.BlockSpec(memory_space=pl.ANY),
            ],
            out_specs=pl.BlockSpec((tm, d), lambda i, dest: (i, 0)),
            scratch_shapes=[pltpu.VMEM((2, top_k, tm, dw), U32), pltpu.SemaphoreType.DMA((2,))],
        ),
        out_shape=jax.ShapeDtypeStruct((t, d), F32),
        compiler_params=_params("arbitrary"),
        name="combine",
    )(dest, x, route, g.reshape(1, d), ys)


def moe_layer(x_new, h_packed, route, counts, w_gate, w_up, w_down, g_final, *, layer, tb, tm_dispatch, tm_combine,
              final_norm):
    t = x_new.shape[0]
    top_k = 2
    cnt = counts[0, :MOE_EXPERTS].astype(I32)
    padded = ((cnt + tb - 1) // tb) * tb
    pends = jnp.cumsum(padded)
    pstarts = pends - padded
    expert_ids = jnp.arange(MOE_EXPERTS, dtype=I32)
    experts = route[:, 0:top_k].astype(I32)
    ranks = route[:, 4:4 + top_k].astype(I32)
    start_of = jnp.sum(jnp.where(experts[..., None] == expert_ids, pstarts, 0), axis=-1)
    dest = (start_of + ranks).reshape(t * top_k)
    rows_out = t * top_k + MOE_EXPERTS * tb
    nb = rows_out // tb
    blk_start = jnp.arange(nb, dtype=I32) * tb
    blk_e = jnp.minimum(jnp.sum((blk_start[:, None] >= pends[None, :]).astype(I32), axis=1), MOE_EXPERTS - 1)
    blk_onehot = blk_e[:, None] == expert_ids
    blk_end = jnp.sum(jnp.where(blk_onehot, pstarts + cnt, 0), axis=1)
    blk_valid = jnp.clip(blk_end - blk_start, 0, tb).astype(I32)
    nused = (pends[-1:] // tb).astype(I32)

    xs = dispatch(dest, h_packed, rows_out=rows_out, tm=tm_dispatch, top_k=top_k)
    ys = expert_ffn(blk_e, blk_valid, nused, xs, w_gate, w_up, w_down, layer=layer, tb=tb)
    return combine(dest, x_new, route, g_final, ys, tm=tm_combine, top_k=top_k, final_norm=final_norm)


def _rope_kernel(pos_ref, inv_ref, cos_ref, sin_ref):
    ang = pos_ref[...].astype(F32) * inv_ref[...]
    cos_ref[...] = jnp.cos(ang)
    sin_ref[...] = jnp.sin(ang)


def rope_table(positions, half, *, tm):
    t = positions.size
    inv_freq = (ROPE_BASE ** (-jnp.arange(half, dtype=F32) / half)).reshape(1, half)
    return pl.pallas_call(
        _rope_kernel,
        grid=(t // tm,),
        in_specs=[pl.BlockSpec((tm, 1), lambda i: (i, 0)), pl.BlockSpec((1, half), lambda i: (0, 0))],
        out_specs=[pl.BlockSpec((tm, half), lambda i: (i, 0))] * 2,
        out_shape=[jax.ShapeDtypeStruct((t, half), F32)] * 2,
        compiler_params=_params("arbitrary"),
        name="rope_table",
    )(positions.reshape(t, 1), inv_freq)


def _retention_kernel(lg_ref, q_ref, k_ref, v_ref, g_ref, cos_ref, sin_ref, gain_ref, o_ref,
                      state_ref, rhs_ref, kscale_ref, eps_ref, *, c, hp):
    dk = q_ref.shape[1] // hp
    dv = v_ref.shape[1] // hp
    half = dk // 2
    heads = range(hp)
    lgs = [lg_ref[pl.program_id(1) * hp + h] for h in heads]
    pos = lax.broadcasted_iota(I32, (c, dk), 0).astype(F32)

    @pl.when(pl.program_id(2) == 0)
    def _():
        state_ref[...] = jnp.zeros_like(state_ref)
        for h in heads:
            rhs_ref[h, c:, :] = jnp.zeros((dk, dv), BF16)
            kscale_ref[h] = jnp.exp(-(pos + 1.0) * lgs[h]) * (dk ** -0.5)
            eps_ref[h] = jnp.exp(-2.0 * (pos[:, 0:1] + 1.0) * lgs[h]) * (dv * EPS)

    cos = cos_ref[...]
    sin = sin_ref[...]

    def rotate(x):
        x1, x2 = x[:, :half], x[:, half:]
        return jnp.concatenate([x1 * cos - x2 * sin, x2 * cos + x1 * sin], axis=1)

    causal = lax.broadcasted_iota(I32, (c, c), 0) >= lax.broadcasted_iota(I32, (c, c), 1)

    def scores(h):
        qr = rotate(q_ref[:, h * dk:(h + 1) * dk].astype(F32)).astype(BF16)
        kp = (rotate(k_ref[:, h * dk:(h + 1) * dk].astype(F32)) * kscale_ref[h]).astype(BF16)
        s = lax.dot_general(qr, kp, (((1,), (1,)), ((), ())), preferred_element_type=F32)
        return qr, kp, s

    ahead = scores(0)
    for h in heads:
        qr, kp, s = ahead
        if h + 1 < hp:
            ahead = scores(h + 1)
        v = v_ref[:, h * dv:(h + 1) * dv]
        rhs_ref[h, :c, :] = v
        kv = lax.dot_general(kp, v, (((0,), (0,)), ((), ())), preferred_element_type=F32)
        s = jnp.where(causal, s, 0.0).astype(BF16)
        y = jnp.dot(jnp.concatenate([s, qr], axis=1), rhs_ref[h], preferred_element_type=F32)

        state = (state_ref[h] + kv) * jnp.exp(jnp.zeros((1, dv), F32) + c * lgs[h])
        state_ref[h] = state
        rhs_ref[h, c:, :] = state.astype(BF16)

        yc = y - jnp.sum(y, axis=-1, keepdims=True) * (1.0 / dv)
        ss = jnp.sum(yc * yc, axis=-1, keepdims=True)
        yn = yc * lax.rsqrt(ss + eps_ref[h]) * (gain_ref[0, h:h + 1, :] * (dv ** 0.5))
        g = g_ref[:, h * dv:(h + 1) * dv]
        gate = g / (1.0 + jnp.exp(-g))
        o_ref[:, h * dv:(h + 1) * dv] = yn.astype(o_ref.dtype) * gate


def retention(proj, cos, sin, gain, *, batch, seq, heads, dk, dv, c, hp):
    nchunk = seq // c
    log_g = jnp.log1p(-jnp.power(2.0, -5.0 - jnp.arange(heads, dtype=F32)))
    groups = heads // hp
    k_col = groups
    v_col = 2 * heads * dk // (hp * dv)
    g_col = v_col + groups
    half = dk // 2

    def rows(b, n):
        return b * nchunk + n

    return pl.pallas_call(
        functools.partial(_retention_kernel, c=c, hp=hp),
        grid_spec=pltpu.PrefetchScalarGridSpec(
            num_scalar_prefetch=1,
            grid=(batch, groups, nchunk),
            in_specs=[
                pl.BlockSpec((c, hp * dk), lambda b, h, n, lg: (rows(b, n), h)),
                pl.BlockSpec((c, hp * dk), lambda b, h, n, lg: (rows(b, n), k_col + h)),
                pl.BlockSpec((c, hp * dv), lambda b, h, n, lg: (rows(b, n), v_col + h)),
                pl.BlockSpec((c, hp * dv), lambda b, h, n, lg: (rows(b, n), g_col + h)),
                pl.BlockSpec((c, half), lambda b, h, n, lg: (rows(b, n), 0)),
                pl.BlockSpec((c, half), lambda b, h, n, lg: (rows(b, n), 0)),
                pl.BlockSpec((1, hp, dv), lambda b, h, n, lg: (h, 0, 0)),
            ],
            out_specs=pl.BlockSpec((c, hp * dv), lambda b, h, n, lg: (rows(b, n), h)),
            scratch_shapes=[
                pltpu.VMEM((hp, dk, dv), F32),
                pltpu.VMEM((hp, c + dk, dv), BF16),
                pltpu.VMEM((hp, c, dk), F32),
                pltpu.VMEM((hp, c, 1), F32),
            ],
        ),
        out_shape=jax.ShapeDtypeStruct((batch * seq, heads * dv), BF16),
        compiler_params=_params("arbitrary", "arbitrary", "arbitrary"),
        name="retention",
    )(log_g, proj, proj, proj, proj, cos, sin, gain.reshape(groups, hp, dv))


def _router_params(wg, bg, we, be):
    d = wg.shape[0]
    n = MOE_GROUPS + MOE_EXPERTS
    w = jnp.concatenate([wg, we, jnp.zeros((d, LANES - n), F32)], axis=1)
    b = jnp.concatenate([bg, be, jnp.zeros((LANES - n,), F32)]).reshape(1, LANES)
    w_hi = w.astype(BF16)
    w_lo = (w - w_hi.astype(F32)).astype(BF16)
    return jnp.concatenate([w_hi, w_lo], axis=1), b


def _pick(total, want):
    t = min(want, total)
    while total % t:
        t //= 2
    return t


def kernel(x, positions, ev_norm, ev_w_in, ev_conv_w, ev_conv_b, ev_gate_a_w, ev_gate_a_b, ev_gate_x_w, ev_gate_x_b, ev_lru_lambda, ev_w_out, od_norm, od_w_in, od_gn_gain, od_w_out, ffn_norm, router_group_w, router_group_b, router_expert_w, router_expert_b, expert_w_gate, expert_w_up, expert_w_down, final_norm):
    batch, seq, d = x.shape
    t = batch * seq
    depth = ffn_norm.shape[0]
    lru_w = ev_conv_w.shape[2]
    sb_w = (ev_w_in.shape[2] - 2 * lru_w) // 3
    head_dim = sb_w // SB_HEADS
    dv = od_gn_gain.shape[2]
    dk = (od_w_in.shape[2] - 2 * RET_HEADS * dv) // (2 * RET_HEADS)

    tm_proj = _pick(t, 1024)
    tm_out = _pick(t, 512)
    tm_dispatch = _pick(t, 2048)
    tm_combine = _pick(t, 512)
    tb = 512

    xf = x.reshape(t, d)
    for layer in range(depth):
        i = layer // 2
        if layer % 2 == 0:
            proj = norm_matmul(xf, ev_norm[i], ev_w_in[i].astype(BF16), tm=tm_proj, tn=_pick(ev_w_in.shape[2], 1024))
            gate_w = jnp.concatenate([ev_gate_a_w[i], ev_gate_x_w[i]], axis=2).astype(BF16)
            gate_b = jnp.stack([ev_gate_a_b[i].reshape(lru_w), ev_gate_x_b[i].reshape(lru_w)])
            y_a = lru_branch(proj, ev_conv_w[i], ev_conv_b[i], gate_w, gate_b, ev_lru_lambda[i],
                             batch=batch, seq=seq, width=lru_w, tc=_pick(seq, 256))
            col0 = 2 * lru_w // head_dim
            y_b = stick_breaking(proj, batch=batch, seq=seq, heads=SB_HEADS, head_dim=head_dim,
                                 q_col=col0, k_col=col0 + SB_HEADS, v_col=col0 + 2 * SB_HEADS,
                                 qb=SBA_QUERY_BLOCK, nchain=SBA_CHAINS, wk=SBA_FIRST_KEYS, fk=SBA_MORE_KEYS)
            w_out = ev_w_out[i].astype(BF16)
            ys, ws = [y_a, y_b], [w_out[:lru_w], w_out[lru_w:]]
        else:
            proj = norm_matmul(xf, od_norm[i], od_w_in[i].astype(BF16), tm=tm_proj, tn=_pick(od_w_in.shape[2], 1024))
            cos, sin = rope_table(positions, dk // 2, tm=_pick(t, 1024))
            y = retention(proj, cos, sin, od_gn_gain[i], batch=batch, seq=seq, heads=RET_HEADS,
                          dk=dk, dv=dv, c=_pick(seq, 256), hp=RET_HEADS_PER_STEP)
            ys, ws = [y], [od_w_out[i].astype(BF16)]

        rw, rb = _router_params(router_group_w[layer], router_group_b[layer],
                                router_expert_w[layer], router_expert_b[layer])
        x_new, h_packed, route, counts = out_router(ys, ws, xf, ffn_norm[layer], rw, rb, tm=tm_out)
        last = layer == depth - 1
        xf = moe_layer(x_new, h_packed, route, counts, expert_w_gate, expert_w_up, expert_w_down, final_norm,
                       layer=layer, tb=tb, tm_dispatch=tm_dispatch, tm_combine=tm_combine, final_norm=last)
    return xf.reshape(batch, seq, d)
```

```python
import functools

import jax
import jax.numpy as jnp
from jax import lax
from jax.experimental import pallas as pl
from jax.experimental.pallas import tpu as pltpu

F32 = jnp.float32
BF16 = jnp.bfloat16
U32 = jnp.uint32
I32 = jnp.int32

LANES = 128
SUBLANES = 8
VMEM_BYTES_V7X = 64 * 1024 * 1024
VMEM_LIMIT = VMEM_BYTES_V7X * 7 // 8

EPS = 1e-6
LRU_BLOCKS = 8
CONV_WIDTH = 4
LRU_C = 8.0
SB_HEADS = 8
RET_HEADS = 8
ROPE_BASE = 10000.0
MOE_GROUPS = 4
MOE_EXPERTS_PER_GROUP = 8
MOE_EXPERTS = MOE_GROUPS * MOE_EXPERTS_PER_GROUP

SBA_QUERY_BLOCK = 64
SBA_CHAINS = 32
SBA_FIRST_KEYS = 256
SBA_MORE_KEYS = 128
LOG2E = 1.4426950408889634

RET_HEADS_PER_STEP = 8

DMA_ISSUE_UNROLL = 8

F32_EXP_ZERO = -104.0

HI16 = 0xFFFF0000


def _params(*sem):
    return pltpu.CompilerParams(dimension_semantics=sem, vmem_limit_bytes=VMEM_LIMIT)


def _resident(shape):
    return pl.BlockSpec(shape, lambda *_: (0,) * len(shape), pipeline_mode=pl.Buffered(1))


def _pack_bf16_pair(a, b):
    ua = lax.bitcast_convert_type(a.astype(BF16).astype(F32), U32)
    ub = lax.bitcast_convert_type(b.astype(BF16).astype(F32), U32)
    return (ua >> 16) | (ub & jnp.uint32(HI16))


def _unpack_bf16_pair(w):
    a = lax.bitcast_convert_type(w << 16, F32)
    b = lax.bitcast_convert_type(w & jnp.uint32(HI16), F32)
    return a, b


def _norm_matmul_kernel(x_ref, g_ref, w_ref, o_ref, h_ref):
    @pl.when(pl.program_id(1) == 0)
    def _():
        x = x_ref[...]
        ms = jnp.mean(x * x, axis=-1, keepdims=True)
        h_ref[...] = (x * lax.rsqrt(ms + EPS) * g_ref[...]).astype(h_ref.dtype)

    o_ref[...] = jnp.dot(h_ref[...], w_ref[...], preferred_element_type=F32).astype(o_ref.dtype)


def norm_matmul(x, g, w, *, tm, tn):
    t, d = x.shape
    n = w.shape[1]
    return pl.pallas_call(
        _norm_matmul_kernel,
        grid=(t // tm, n // tn),
        in_specs=[
            pl.BlockSpec((tm, d), lambda i, j: (i, 0)),
            pl.BlockSpec((1, d), lambda i, j: (0, 0)),
            pl.BlockSpec((d, tn), lambda i, j: (0, j)),
        ],
        out_specs=pl.BlockSpec((tm, tn), lambda i, j: (i, j)),
        out_shape=jax.ShapeDtypeStruct((t, n), BF16),
        scratch_shapes=[pltpu.VMEM((tm, d), BF16)],
        compiler_params=_params("arbitrary", "arbitrary"),
        name="norm_matmul",
    )(x, g.reshape(1, d), w)


def _sigmoid(x):
    return 1.0 / (1.0 + jnp.exp(-x))


def _lru_kernel(xr_ref, gr_ref, cw_ref, cb_ref, gw_ref, gb_ref, lam_ref, o_ref, tail_ref, h_ref, *, tc):
    @pl.when(pl.program_id(1) == 0)
    def _():
        tail_ref[...] = jnp.zeros_like(tail_ref)
        h_ref[...] = jnp.zeros_like(h_ref)

    c = xr_ref.shape[1]
    bw = c // LRU_BLOCKS
    x = xr_ref[...].astype(F32)
    xe = jnp.concatenate([tail_ref[...], x], axis=0)
    cw = cw_ref[...]
    xc = cb_ref[...] + cw[3:4] * x
    for k in range(CONV_WIDTH - 1):
        off = SUBLANES - (CONV_WIDTH - 1 - k)
        xc = xc + cw[k:k + 1] * xe[off:off + tc]
    tail_ref[...] = x[tc - SUBLANES:, :]

    xcb = xc.astype(BF16)
    parts = [jnp.dot(xcb[:, g * bw:(g + 1) * bw], gw_ref[g], preferred_element_type=F32)
             for g in range(LRU_BLOCKS)]
    gb = gb_ref[...]
    r = _sigmoid(jnp.concatenate([p[:, :bw] for p in parts], axis=1) + gb[0:1])
    ig = _sigmoid(jnp.concatenate([p[:, bw:] for p in parts], axis=1) + gb[1:2])

    nlam = -lam_ref[...]
    softplus_nlam = jnp.maximum(nlam, 0.0) + jnp.log1p(jnp.exp(-jnp.abs(nlam)))
    log_a = (-LRU_C) * r * softplus_nlam
    a = jnp.exp(log_a)
    u = jnp.sqrt(1.0 - a * a) * (ig * xc)

    groups = tc // SUBLANES
    a = a.reshape(groups, SUBLANES, c)
    u = u.reshape(groups, SUBLANES, c)
    row_in_group = lax.broadcasted_iota(I32, (groups, SUBLANES, c), 1)
    d = 1
    while d < SUBLANES:
        keep = row_in_group >= d
        a_prev = jnp.where(keep, pltpu.roll(a, d, 1), 1.0)
        u_prev = jnp.where(keep, pltpu.roll(u, d, 1), 0.0)
        u = a * u_prev + u
        a = a * a_prev
        d *= 2
    carry = h_ref[...]
    hs = []
    for g in range(groups):
        hg = u[g] + a[g] * carry
        carry = hg[SUBLANES - 1:SUBLANES, :]
        hs.append(hg)
    h = jnp.concatenate(hs, axis=0)
    h_ref[...] = carry
    o_ref[...] = (h * jax.nn.gelu(gr_ref[...].astype(F32))).astype(o_ref.dtype)


def lru_branch(proj, conv_w, conv_b, gate_w, gate_b, lam, *, batch, seq, width, tc):
    nchunk = seq // tc
    return pl.pallas_call(
        functools.partial(_lru_kernel, tc=tc),
        grid=(batch, nchunk),
        in_specs=[
            pl.BlockSpec((tc, width), lambda b, i: (b * nchunk + i, 0)),
            pl.BlockSpec((tc, width), lambda b, i: (b * nchunk + i, 1)),
            pl.BlockSpec((CONV_WIDTH, width), lambda b, i: (0, 0)),
            pl.BlockSpec((1, width), lambda b, i: (0, 0)),
            pl.BlockSpec(gate_w.shape, lambda b, i: (0, 0, 0)),
            pl.BlockSpec((2, width), lambda b, i: (0, 0)),
            pl.BlockSpec((1, width), lambda b, i: (0, 0)),
        ],
        out_specs=pl.BlockSpec((tc, width), lambda b, i: (b * nchunk + i, 0)),
        out_shape=jax.ShapeDtypeStruct((batch * seq, width), BF16),
        scratch_shapes=[pltpu.VMEM((SUBLANES, width), F32), pltpu.VMEM((1, width), F32)],
        compiler_params=_params("arbitrary", "arbitrary"),
        name="lru",
    )(proj, proj, conv_w, conv_b.reshape(1, width), gate_w, gate_b, lam.reshape(1, width))


def _sba_blocks(qs, kbs, vbs, later, masks, runs, *, scale):
    n = len(qs)
    qb = qs[0].shape[0]
    zs = [lax.dot_general(q, kb, (((1,), (1,)), ((), ())), preferred_element_type=F32) * (scale * LOG2E)
          for q, kb in zip(qs, kbs)]
    softplus = [jnp.maximum(z, 0.0) + jnp.log(1.0 + jnp.exp2(-jnp.abs(z))) * LOG2E for z in zs]
    log_1m = [jnp.where(m, -s, 0.0) for m, s in zip(masks, softplus)]
    his = [x.astype(BF16) for x in log_1m]
    los = [(x - h.astype(F32)).astype(BF16) for x, h in zip(log_1m, his)]
    sums = jnp.dot(jnp.concatenate(his + los, axis=0), later, preferred_element_type=F32)
    outs = []
    for c in range(n):
        rest = sums[c * qb:(c + 1) * qb] + sums[(n + c) * qb:(n + c + 1) * qb]
        log_w = (zs[c] - softplus[c]) + rest
        if runs[c] is not None:
            log_w = log_w + runs[c]
        w = jnp.where(masks[c], jnp.exp2(log_w), 0.0)
        outs.append((w.astype(BF16), rest[:, 0:1] + log_1m[c][:, 0:1]))
    return [(jnp.dot(w, vb, preferred_element_type=F32), total) for (w, total), vb in zip(outs, vbs)]


def _sba_kernel(q_ref, k_ref, v_ref, o_ref, later_ref, acc_ref, run_ref, *, qb, nchain, wk, fk, scale):
    step_id = pl.program_id(2)
    dead_below = F32_EXP_ZERO * LOG2E

    @pl.when(step_id == 0)
    def _():
        r = lax.broadcasted_iota(I32, (wk, wk), 0)
        c = lax.broadcasted_iota(I32, (wk, wk), 1)
        later_ref[...] = jnp.where(r > c, 1.0, 0.0).astype(BF16)

    col_minus_row = (lax.broadcasted_iota(I32, (qb, wk), 1) - lax.broadcasted_iota(I32, (qb, wk), 0))

    def first_key(c):
        q0 = (step_id * nchain + c) * qb
        return q0, pl.multiple_of(jnp.maximum(q0 + qb - wk, 0), qb)

    lefts = []
    masks = []
    for c in range(nchain):
        q0, left = first_key(c)
        lefts.append(left)
        masks.append(col_minus_row < q0 - left)
    first = _sba_blocks([q_ref[c * qb:(c + 1) * qb, :] for c in range(nchain)],
                        [k_ref[pl.ds(left, wk), :] for left in lefts],
                        [v_ref[pl.ds(left, wk), :] for left in lefts],
                        later_ref[...], masks, [None] * nchain, scale=scale)
    worst = first[0][1]
    for c, (out, run) in enumerate(first):
        o_ref[c * qb:(c + 1) * qb, :] = out.astype(o_ref.dtype)
        worst = jnp.maximum(worst, run)

    @pl.when(jnp.max(worst) > dead_below)
    def _():
        for c, (out, run) in enumerate(first):
            acc_ref[c] = out
            run_ref[c] = run
        key_col = lax.broadcasted_iota(I32, (qb, fk), 1)

        def walk_left(c, carry):
            rows = pl.ds(pl.multiple_of(c * qb, qb), qb)

            def cond(state):
                left, live = state
                return jnp.logical_and(left > 0, live)

            def body(state):
                left, _ = state
                start = pl.multiple_of(jnp.maximum(left - fk, 0), qb)
                run = run_ref[c]
                (out, more), = _sba_blocks([q_ref[rows, :]], [k_ref[pl.ds(start, fk), :]],
                                           [v_ref[pl.ds(start, fk), :]], later_ref[0:fk, 0:fk],
                                           [key_col < left - start], [run], scale=scale)
                acc_ref[c] += out
                run_ref[c] = run + more
                return start, jnp.max(run + more) > dead_below

            lax.while_loop(cond, body, (first_key(c)[1], jnp.max(run_ref[c]) > dead_below))
            o_ref[rows, :] = acc_ref[c].astype(o_ref.dtype)
            return carry

        lax.fori_loop(0, nchain, walk_left, 0)


def stick_breaking(proj, *, batch, seq, heads, head_dim, q_col, k_col, v_col, qb, nchain, wk, fk):
    nstep = seq // (qb * nchain)
    rows = qb * nchain
    return pl.pallas_call(
        functools.partial(_sba_kernel, qb=qb, nchain=nchain, wk=wk, fk=fk, scale=head_dim ** -0.5),
        grid=(batch, heads, nstep),
        in_specs=[
            pl.BlockSpec((rows, head_dim), lambda b, h, i: (b * nstep + i, q_col + h)),
            pl.BlockSpec((seq, head_dim), lambda b, h, i: (b, k_col + h)),
            pl.BlockSpec((seq, head_dim), lambda b, h, i: (b, v_col + h)),
        ],
        out_specs=pl.BlockSpec((rows, head_dim), lambda b, h, i: (b * nstep + i, h)),
        out_shape=jax.ShapeDtypeStruct((batch * seq, heads * head_dim), BF16),
        scratch_shapes=[
            pltpu.VMEM((wk, wk), BF16),
            pltpu.VMEM((nchain, qb, head_dim), F32),
            pltpu.VMEM((nchain, qb, 1), F32),
        ],
        compiler_params=_params("arbitrary", "arbitrary", "arbitrary"),
        name="sba",
    )(proj, proj, proj)


def _out_router_kernel(*refs, n_in):
    y_refs = refs[:n_in]
    w_refs = refs[n_in:2 * n_in]
    x_ref, g_ref, rw_ref, rb_ref = refs[2 * n_in:2 * n_in + 4]
    xo_ref, hp_ref, route_ref, cnt_ref, carry_ref = refs[2 * n_in + 4:]

    @pl.when(pl.program_id(0) == 0)
    def _():
        carry_ref[...] = jnp.zeros_like(carry_ref)

    acc = x_ref[...]
    for y_ref, w_ref in zip(y_refs, w_refs):
        acc = acc + jnp.dot(y_ref[...], w_ref[...], preferred_element_type=F32)
    xo_ref[...] = acc

    tm, d = acc.shape
    ms = jnp.mean(acc * acc, axis=-1, keepdims=True)
    h = acc * lax.rsqrt(ms + EPS) * g_ref[...]
    hp_ref[...] = _pack_bf16_pair(h[:, :d // 2], h[:, d // 2:])

    h_hi = h.astype(BF16)
    h_lo = (h - h_hi.astype(F32)).astype(BF16)
    logits = (jnp.dot(h_hi, rw_ref[:, :LANES], preferred_element_type=F32)
              + jnp.dot(h_lo, rw_ref[:, :LANES], preferred_element_type=F32)
              + jnp.dot(h_hi, rw_ref[:, LANES:], preferred_element_type=F32)
              + rb_ref[...])

    lane = lax.broadcasted_iota(I32, (tm, LANES), 1)
    neg = jnp.float32(-jnp.inf)
    big = jnp.int32(LANES)

    def first_argmax(vals, mask):
        m = jnp.max(jnp.where(mask, vals, neg), axis=-1, keepdims=True)
        idx = jnp.min(jnp.where(jnp.logical_and(mask, vals == m), lane, big), axis=-1, keepdims=True)
        return m, idx

    is_group = lane < MOE_GROUPS
    g_max, g_idx = first_argmax(logits, is_group)
    g_w = 1.0 / jnp.sum(jnp.where(is_group, jnp.exp(logits - g_max), 0.0), axis=-1, keepdims=True)
    e_lo = MOE_GROUPS + g_idx * MOE_EXPERTS_PER_GROUP
    in_group = jnp.logical_and(lane >= e_lo, lane < e_lo + MOE_EXPERTS_PER_GROUP)
    l1, i1 = first_argmax(logits, in_group)
    l2, i2 = first_argmax(logits, jnp.logical_and(in_group, lane != i1))
    t = jnp.exp(l2 - l1)
    w1 = g_w / (1.0 + t)
    w2 = g_w * t / (1.0 + t)
    e1 = i1 - MOE_GROUPS
    e2 = i2 - MOE_GROUPS

    onehot = jnp.where(jnp.logical_or(lane == e1, lane == e2), 1.0, 0.0)
    lower = jnp.where(lax.broadcasted_iota(I32, (tm, tm), 1) < lax.broadcasted_iota(I32, (tm, tm), 0),
                      1.0, 0.0).astype(BF16)
    before = jnp.dot(lower, onehot.astype(BF16), preferred_element_type=F32) + carry_ref[...]
    rank1 = jnp.sum(jnp.where(lane == e1, before, 0.0), axis=-1, keepdims=True)
    rank2 = jnp.sum(jnp.where(lane == e2, before, 0.0), axis=-1, keepdims=True)
    new_carry = before[tm - 1:tm, :] + onehot[tm - 1:tm, :]
    carry_ref[...] = new_carry
    cnt_ref[...] = jnp.broadcast_to(new_carry, cnt_ref.shape)

    route = jnp.where(lane == 0, e1.astype(F32), 0.0)
    route = jnp.where(lane == 1, e2.astype(F32), route)
    route = jnp.where(lane == 2, w1, route)
    route = jnp.where(lane == 3, w2, route)
    route = jnp.where(lane == 4, rank1, route)
    route = jnp.where(lane == 5, rank2, route)
    route_ref[...] = route


def out_router(ys, ws, x, g, rw, rb, *, tm):
    t, d = x.shape
    n_in = len(ys)
    in_specs = ([pl.BlockSpec((tm, y.shape[1]), lambda i: (i, 0)) for y in ys]
                + [_resident(w.shape) for w in ws]
                + [pl.BlockSpec((tm, d), lambda i: (i, 0)),
                   _resident((1, d)), _resident(rw.shape), _resident((1, LANES))])
    return pl.pallas_call(
        functools.partial(_out_router_kernel, n_in=n_in),
        grid=(t // tm,),
        in_specs=in_specs,
        out_specs=[
            pl.BlockSpec((tm, d), lambda i: (i, 0)),
            pl.BlockSpec((tm, d // 2), lambda i: (i, 0)),
            pl.BlockSpec((tm, LANES), lambda i: (i, 0)),
            pl.BlockSpec((SUBLANES, LANES), lambda i: (0, 0)),
        ],
        out_shape=[
            jax.ShapeDtypeStruct((t, d), F32),
            jax.ShapeDtypeStruct((t, d // 2), U32),
            jax.ShapeDtypeStruct((t, LANES), F32),
            jax.ShapeDtypeStruct((SUBLANES, LANES), F32),
        ],
        scratch_shapes=[pltpu.VMEM((1, LANES), F32)],
        compiler_params=_params("arbitrary"),
        name="out_router",
    )(*ys, *ws, x, g.reshape(1, d), rw, rb)


def _dispatch_kernel(dest_ref, h_ref, xs_ref, sem, *, tm, top_k):
    base = pl.program_id(0) * tm * top_k

    def row_copy(t, k):
        return pltpu.make_async_copy(h_ref.at[pl.ds(t, 1)],
                                     xs_ref.at[pl.ds(dest_ref[base + t * top_k + k], 1)], sem)

    def issue(t, carry):
        for k in range(top_k):
            row_copy(t, k).start()
        return carry

    lax.fori_loop(0, tm, issue, 0, unroll=DMA_ISSUE_UNROLL)

    for k in range(top_k):
        pltpu.make_async_copy(h_ref, xs_ref.at[pl.ds(0, tm)], sem).wait()


def dispatch(dest, h_packed, *, rows_out, tm, top_k):
    t, dw = h_packed.shape
    return pl.pallas_call(
        functools.partial(_dispatch_kernel, tm=tm, top_k=top_k),
        grid_spec=pltpu.PrefetchScalarGridSpec(
            num_scalar_prefetch=1,
            grid=(t // tm,),
            in_specs=[pl.BlockSpec((tm, dw), lambda i, dest: (i, 0))],
            out_specs=pl.BlockSpec(memory_space=pl.ANY),
            scratch_shapes=[pltpu.SemaphoreType.DMA(())],
        ),
        out_shape=jax.ShapeDtypeStruct((rows_out, dw), U32),
        compiler_params=pltpu.CompilerParams(dimension_semantics=("arbitrary",),
                                             vmem_limit_bytes=VMEM_LIMIT, has_side_effects=True),
        name="dispatch",
    )(dest, h_packed)


def _ffn_kernel(blk_e_ref, blk_valid_ref, nused_ref, x_ref, wg_ref, wu_ref, wd_ref, y_ref,
                wg_bf, wu_bf, wd_bf):
    j = pl.program_id(0)

    @pl.when(j < nused_ref[0])
    def _():
        @pl.when(jnp.logical_or(j == 0, blk_e_ref[j] != blk_e_ref[jnp.maximum(j - 1, 0)]))
        def _():
            wg_bf[...] = wg_ref[0, 0].astype(BF16)
            wu_bf[...] = wu_ref[0, 0].astype(BF16)
            wd_bf[...] = wd_ref[0, 0].astype(BF16)

        rows = lax.broadcasted_iota(I32, x_ref.shape, 0)
        packed = jnp.where(rows < blk_valid_ref[j], x_ref[...], jnp.uint32(0))
        a, b = _unpack_bf16_pair(packed)
        x = jnp.concatenate([a.astype(BF16), b.astype(BF16)], axis=1)
        gate = jnp.dot(x, wg_bf[...], preferred_element_type=F32)
        up = jnp.dot(x, wu_bf[...], preferred_element_type=F32)
        hid = (gate * _sigmoid(gate) * up).astype(BF16)
        y = jnp.dot(hid, wd_bf[...], preferred_element_type=F32)
        half = y.shape[1] // 2
        y_ref[...] = _pack_bf16_pair(y[:, :half], y[:, half:])


def expert_ffn(blk_e, blk_valid, nused, xs, w_gate, w_up, w_down, *, layer, tb):
    p, dw = xs.shape
    _, _, d, ff = w_gate.shape

    def row_map(j, be, bv, nu):
        return (jnp.minimum(j, nu[0] - 1), 0)

    def w_map(j, be, bv, nu):
        return (layer, be[j], 0, 0)

    return pl.pallas_call(
        _ffn_kernel,
        grid_spec=pltpu.PrefetchScalarGridSpec(
            num_scalar_prefetch=3,
            grid=(p // tb,),
            in_specs=[
                pl.BlockSpec((tb, dw), row_map),
                pl.BlockSpec((1, 1, d, ff), w_map),
                pl.BlockSpec((1, 1, d, ff), w_map),
                pl.BlockSpec((1, 1, ff, d), w_map),
            ],
            out_specs=pl.BlockSpec((tb, dw), row_map),
            scratch_shapes=[pltpu.VMEM((d, ff), BF16), pltpu.VMEM((d, ff), BF16), pltpu.VMEM((ff, d), BF16)],
        ),
        out_shape=jax.ShapeDtypeStruct((p, dw), U32),
        compiler_params=_params("arbitrary"),
        name="ffn",
    )(blk_e, blk_valid, nused, xs, w_gate, w_up, w_down)


def _combine_kernel(dest_ref, x_ref, route_ref, g_ref, ys_ref, o_ref, buf_ref, sem, *, tm, top_k, final_norm):
    i = pl.program_id(0)
    slot = i % 2

    def gather_tile(tile, slot):
        base = tile * tm * top_k

        def issue(t, carry):
            for k in range(top_k):
                pltpu.make_async_copy(ys_ref.at[pl.ds(dest_ref[base + t * top_k + k], 1)],
                                      buf_ref.at[slot, k, pl.ds(t, 1)], sem.at[slot]).start()
            return carry

        lax.fori_loop(0, tm, issue, 0, unroll=DMA_ISSUE_UNROLL)

    @pl.when(i == 0)
    def _():
        gather_tile(i, slot)

    @pl.when(i + 1 < pl.num_programs(0))
    def _():
        gather_tile(i + 1, 1 - slot)

    for k in range(top_k):
        pltpu.make_async_copy(ys_ref.at[pl.ds(0, tm)], buf_ref.at[slot, k], sem.at[slot]).wait()

    route = route_ref[...]
    out = x_ref[...]
    for k in range(top_k):
        a, b = _unpack_bf16_pair(buf_ref[slot, k])
        out = out + route[:, 2 + k:3 + k] * jnp.concatenate([a, b], axis=1)
    if final_norm:
        ms = jnp.mean(out * out, axis=-1, keepdims=True)
        out = out * lax.rsqrt(ms + EPS) * g_ref[...]
    o_ref[...] = out


def combine(dest, x, route, g, ys, *, tm, top_k, final_norm):
    t, d = x.shape
    dw = ys.shape[1]
    return pl.pallas_call(
        functools.partial(_combine_kernel, tm=tm, top_k=top_k, final_norm=final_norm),
        grid_spec=pltpu.PrefetchScalarGridSpec(
            num_scalar_prefetch=1,
            grid=(t // tm,),
            in_specs=[
                pl.BlockSpec((tm, d), lambda i, dest: (i, 0)),
                pl.BlockSpec((tm, LANES), lambda i, dest: (i, 0)),
                pl.BlockSpec((1, d), lambda i, dest: (0, 0)),
                pl.BlockSpec(memory_space=pl.ANY),
            ],
            out_specs=pl.BlockSpec((tm, d), lambda i, dest: (i, 0)),
            scratch_shapes=[pltpu.VMEM((2, top_k, tm, dw), U32), pltpu.SemaphoreType.DMA((2,))],
        ),
        out_shape=jax.ShapeDtypeStruct((t, d), F32),
        compiler_params=_params("arbitrary"),
        name="combine",
    )(dest, x, route, g.reshape(1, d), ys)


def moe_layer(x_new, h_packed, route, counts, w_gate, w_up, w_down, g_final, *, layer, tb, tm_dispatch, tm_combine,
              final_norm):
    t = x_new.shape[0]
    top_k = 2
    cnt = counts[0, :MOE_EXPERTS].astype(I32)
    padded = ((cnt + tb - 1) // tb) * tb
    pends = jnp.cumsum(padded)
    pstarts = pends - padded
    expert_ids = jnp.arange(MOE_EXPERTS, dtype=I32)
    experts = route[:, 0:top_k].astype(I32)
    ranks = route[:, 4:4 + top_k].astype(I32)
    start_of = jnp.sum(jnp.where(experts[..., None] == expert_ids, pstarts, 0), axis=-1)
    dest = (start_of + ranks).reshape(t * top_k)
    rows_out = t * top_k + MOE_EXPERTS * tb
    nb = rows_out // tb
    blk_start = jnp.arange(nb, dtype=I32) * tb
    blk_e = jnp.minimum(jnp.sum((blk_start[:, None] >= pends[None, :]).astype(I32), axis=1), MOE_EXPERTS - 1)
    blk_onehot = blk_e[:, None] == expert_ids
    blk_end = jnp.sum(jnp.where(blk_onehot, pstarts + cnt, 0), axis=1)
    blk_valid = jnp.clip(blk_end - blk_start, 0, tb).astype(I32)
    nused = (pends[-1:] // tb).astype(I32)

    xs = dispatch(dest, h_packed, rows_out=rows_out, tm=tm_dispatch, top_k=top_k)
    ys = expert_ffn(blk_e, blk_valid, nused, xs, w_gate, w_up, w_down, layer=layer, tb=tb)
    return combine(dest, x_new, route, g_final, ys, tm=tm_combine, top_k=top_k, final_norm=final_norm)


def _rope_kernel(pos_ref, inv_ref, cos_ref, sin_ref):
    ang = pos_ref[...].astype(F32) * inv_ref[...]
    cos_ref[...] = jnp.cos(ang)
    sin_ref[...] = jnp.sin(ang)


def rope_table(positions, half, *, tm):
    t = positions.size
    inv_freq = (ROPE_BASE ** (-jnp.arange(half, dtype=F32) / half)).reshape(1, half)
    return pl.pallas_call(
        _rope_kernel,
        grid=(t // tm,),
        in_specs=[pl.BlockSpec((tm, 1), lambda i: (i, 0)), pl.BlockSpec((1, half), lambda i: (0, 0))],
        out_specs=[pl.BlockSpec((tm, half), lambda i: (i, 0))] * 2,
        out_shape=[jax.ShapeDtypeStruct((t, half), F32)] * 2,
        compiler_params=_params("arbitrary"),
        name="rope_table",
    )(positions.reshape(t, 1), inv_freq)


def _retention_kernel(lg_ref, q_ref, k_ref, v_ref, g_ref, cos_ref, sin_ref, gain_ref, o_ref,
                      state_ref, rhs_ref, kscale_ref, eps_ref, *, c, hp):
    dk = q_ref.shape[1] // hp
    dv = v_ref.shape[1] // hp
    half = dk // 2
    heads = range(hp)
    lgs = [lg_ref[pl.program_id(1) * hp + h] for h in heads]
    pos = lax.broadcasted_iota(I32, (c, dk), 0).astype(F32)

    @pl.when(pl.program_id(2) == 0)
    def _():
        state_ref[...] = jnp.zeros_like(state_ref)
        for h in heads:
            rhs_ref[h, c:, :] = jnp.zeros((dk, dv), BF16)
            kscale_ref[h] = jnp.exp(-(pos + 1.0) * lgs[h]) * (dk ** -0.5)
            eps_ref[h] = jnp.exp(-2.0 * (pos[:, 0:1] + 1.0) * lgs[h]) * (dv * EPS)

    cos = cos_ref[...]
    sin = sin_ref[...]

    def rotate(x):
        x1, x2 = x[:, :half], x[:, half:]
        return jnp.concatenate([x1 * cos - x2 * sin, x2 * cos + x1 * sin], axis=1)

    causal = lax.broadcasted_iota(I32, (c, c), 0) >= lax.broadcasted_iota(I32, (c, c), 1)

    def scores(h):
        qr = rotate(q_ref[:, h * dk:(h + 1) * dk].astype(F32)).astype(BF16)
        kp = (rotate(k_ref[:, h * dk:(h + 1) * dk].astype(F32)) * kscale_ref[h]).astype(BF16)
        s = lax.dot_general(qr, kp, (((1,), (1,)), ((), ())), preferred_element_type=F32)
        return qr, kp, s

    ahead = scores(0)
    for h in heads:
        qr, kp, s = ahead
        if h + 1 < hp:
            ahead = scores(h + 1)
        v = v_ref[:, h * dv:(h + 1) * dv]
        rhs_ref[h, :c, :] = v
        kv = lax.dot_general(kp, v, (((0,), (0,)), ((), ())), preferred_element_type=F32)
        s = jnp.where(causal, s, 0.0).astype(BF16)
        y = jnp.dot(jnp.concatenate([s, qr], axis=1), rhs_ref[h], preferred_element_type=F32)

        state = (state_ref[h] + kv) * jnp.exp(jnp.zeros((1, dv), F32) + c * lgs[h])
        state_ref[h] = state
        rhs_ref[h, c:, :] = state.astype(BF16)

        yc = y - jnp.sum(y, axis=-1, keepdims=True) * (1.0 / dv)
        ss = jnp.sum(yc * yc, axis=-1, keepdims=True)
        yn = yc * lax.rsqrt(ss + eps_ref[h]) * (gain_ref[0, h:h + 1, :] * (dv ** 0.5))
        g = g_ref[:, h * dv:(h + 1) * dv]
        gate = g / (1.0 + jnp.exp(-g))
        o_ref[:, h * dv:(h + 1) * dv] = yn.astype(o_ref.dtype) * gate


def retention(proj, cos, sin, gain, *, batch, seq, heads, dk, dv, c, hp):
    nchunk = seq // c
    log_g = jnp.log1p(-jnp.power(2.0, -5.0 - jnp.arange(heads, dtype=F32)))
    groups = heads // hp
    k_col = groups
    v_col = 2 * heads * dk // (hp * dv)
    g_col = v_col + groups
    half = dk // 2

    def rows(b, n):
        return b * nchunk + n

    return pl.pallas_call(
        functools.partial(_retention_kernel, c=c, hp=hp),
        grid_spec=pltpu.PrefetchScalarGridSpec(
            num_scalar_prefetch=1,
            grid=(batch, groups, nchunk),
            in_specs=[
                pl.BlockSpec((c, hp * dk), lambda b, h, n, lg: (rows(b, n), h)),
                pl.BlockSpec((c, hp * dk), lambda b, h, n, lg: (rows(b, n), k_col + h)),
                pl.BlockSpec((c, hp * dv), lambda b, h, n, lg: (rows(b, n), v_col + h)),
                pl.BlockSpec((c, hp * dv), lambda b, h, n, lg: (rows(b, n), g_col + h)),
                pl.BlockSpec((c, half), lambda b, h, n, lg: (rows(b, n), 0)),
                pl.BlockSpec((c, half), lambda b, h, n, lg: (rows(b, n), 0)),
                pl.BlockSpec((1, hp, dv), lambda b, h, n, lg: (h, 0, 0)),
            ],
            out_specs=pl.BlockSpec((c, hp * dv), lambda b, h, n, lg: (rows(b, n), h)),
            scratch_shapes=[
                pltpu.VMEM((hp, dk, dv), F32),
                pltpu.VMEM((hp, c + dk, dv), BF16),
                pltpu.VMEM((hp, c, dk), F32),
                pltpu.VMEM((hp, c, 1), F32),
            ],
        ),
        out_shape=jax.ShapeDtypeStruct((batch * seq, heads * dv), BF16),
        compiler_params=_params("arbitrary", "arbitrary", "arbitrary"),
        name="retention",
    )(log_g, proj, proj, proj, proj, cos, sin, gain.reshape(groups, hp, dv))


def _router_params(wg, bg, we, be):
    d = wg.shape[0]
    n = MOE_GROUPS + MOE_EXPERTS
    w = jnp.concatenate([wg, we, jnp.zeros((d, LANES - n), F32)], axis=1)
    b = jnp.concatenate([bg, be, jnp.zeros((LANES - n,), F32)]).reshape(1, LANES)
    w_hi = w.astype(BF16)
    w_lo = (w - w_hi.astype(F32)).astype(BF16)
    return jnp.concatenate([w_hi, w_lo], axis=1), b


def _pick(total, want):
    t = min(want, total)
    while total % t:
        t //= 2
    return t


def kernel(x, positions, ev_norm, ev_w_in, ev_conv_w, ev_conv_b, ev_gate_a_w, ev_gate_a_b, ev_gate_x_w, ev_gate_x_b, ev_lru_lambda, ev_w_out, od_norm, od_w_in, od_gn_gain, od_w_out, ffn_norm, router_group_w, router_group_b, router_expert_w, router_expert_b, expert_w_gate, expert_w_up, expert_w_down, final_norm):
    batch, seq, d = x.shape
    t = batch * seq
    depth = ffn_norm.shape[0]
    lru_w = ev_conv_w.shape[2]
    sb_w = (ev_w_in.shape[2] - 2 * lru_w) // 3
    head_dim = sb_w // SB_HEADS
    dv = od_gn_gain.shape[2]
    dk = (od_w_in.shape[2] - 2 * RET_HEADS * dv) // (2 * RET_HEADS)

    tm_proj = _pick(t, 1024)
    tm_out = _pick(t, 512)
    tm_dispatch = _pick(t, 2048)
    tm_combine = _pick(t, 512)
    tb = 512

    xf = x.reshape(t, d)
    for layer in range(depth):
        i = layer // 2
        if layer % 2 == 0:
            proj = norm_matmul(xf, ev_norm[i], ev_w_in[i].astype(BF16), tm=tm_proj, tn=_pick(ev_w_in.shape[2], 1024))
            gate_w = jnp.concatenate([ev_gate_a_w[i], ev_gate_x_w[i]], axis=2).astype(BF16)
            gate_b = jnp.stack([ev_gate_a_b[i].reshape(lru_w), ev_gate_x_b[i].reshape(lru_w)])
            y_a = lru_branch(proj, ev_conv_w[i], ev_conv_b[i], gate_w, gate_b, ev_lru_lambda[i],
                             batch=batch, seq=seq, width=lru_w, tc=_pick(seq, 512))
            col0 = 2 * lru_w // head_dim
            y_b = stick_breaking(proj, batch=batch, seq=seq, heads=SB_HEADS, head_dim=head_dim,
                                 q_col=col0, k_col=col0 + SB_HEADS, v_col=col0 + 2 * SB_HEADS,
                                 qb=SBA_QUERY_BLOCK, nchain=_pick(seq // SBA_QUERY_BLOCK, SBA_CHAINS),
                                 wk=SBA_FIRST_KEYS, fk=SBA_MORE_KEYS)
            w_out = ev_w_out[i].astype(BF16)
            ys, ws = [y_a, y_b], [w_out[:lru_w], w_out[lru_w:]]
        else:
            proj = norm_matmul(xf, od_norm[i], od_w_in[i].astype(BF16), tm=tm_proj, tn=_pick(od_w_in.shape[2], 2048))
            cos, sin = rope_table(positions, dk // 2, tm=_pick(t, 1024))
            y = retention(proj, cos, sin, od_gn_gain[i], batch=batch, seq=seq, heads=RET_HEADS,
                          dk=dk, dv=dv, c=_pick(seq, 256), hp=RET_HEADS_PER_STEP)
            ys, ws = [y], [od_w_out[i].astype(BF16)]

        rw, rb = _router_params(router_group_w[layer], router_group_b[layer],
                                router_expert_w[layer], router_expert_b[layer])
        x_new, h_packed, route, counts = out_router(ys, ws, xf, ffn_norm[layer], rw, rb, tm=tm_out)
        last = layer == depth - 1
        xf = moe_layer(x_new, h_packed, route, counts, expert_w_gate, expert_w_up, expert_w_down, final_norm,
                       layer=layer, tb=tb, tm_dispatch=tm_dispatch, tm_combine=tm_combine, final_norm=last)
    return xf.reshape(batch, seq, d)
```

```python
import functools

import jax
import jax.numpy as jnp
from jax import lax
from jax.experimental import pallas as pl
from jax.experimental.pallas import tpu as pltpu

F32 = jnp.float32
BF16 = jnp.bfloat16
U32 = jnp.uint32
I32 = jnp.int32

LANES = 128
SUBLANES = 8
VMEM_BYTES_V7X = 64 * 1024 * 1024
VMEM_LIMIT = VMEM_BYTES_V7X * 7 // 8

EPS = 1e-6
LRU_BLOCKS = 8
CONV_WIDTH = 4
LRU_C = 8.0
SB_HEADS = 8
RET_HEADS = 8
ROPE_BASE = 10000.0
MOE_GROUPS = 4
MOE_EXPERTS_PER_GROUP = 8
MOE_EXPERTS = MOE_GROUPS * MOE_EXPERTS_PER_GROUP

SBA_QUERY_BLOCK = 64
SBA_CHAINS = 32
SBA_FIRST_KEYS = 256
SBA_MORE_KEYS = 128
LOG2E = 1.4426950408889634

RET_HEADS_PER_STEP = 8

DMA_ISSUE_UNROLL = 8
DMA_PRIORITIES = 2

F32_EXP_ZERO = -104.0

HI16 = 0xFFFF0000


def _params(*sem):
    return pltpu.CompilerParams(dimension_semantics=sem, vmem_limit_bytes=VMEM_LIMIT)


def _resident(shape):
    return pl.BlockSpec(shape, lambda *_: (0,) * len(shape), pipeline_mode=pl.Buffered(1))


def _pack_bf16_pair(a, b):
    ua = lax.bitcast_convert_type(a.astype(BF16).astype(F32), U32)
    ub = lax.bitcast_convert_type(b.astype(BF16).astype(F32), U32)
    return (ua >> 16) | (ub & jnp.uint32(HI16))


def _unpack_bf16_pair(w):
    a = lax.bitcast_convert_type(w << 16, F32)
    b = lax.bitcast_convert_type(w & jnp.uint32(HI16), F32)
    return a, b


def _norm_matmul_kernel(x_ref, g_ref, w_ref, o_ref, h_ref):
    @pl.when(pl.program_id(1) == 0)
    def _():
        x = x_ref[...]
        ms = jnp.mean(x * x, axis=-1, keepdims=True)
        h_ref[...] = (x * lax.rsqrt(ms + EPS) * g_ref[...]).astype(h_ref.dtype)

    o_ref[...] = jnp.dot(h_ref[...], w_ref[...], preferred_element_type=F32).astype(o_ref.dtype)


def norm_matmul(x, g, w, *, tm, tn):
    t, d = x.shape
    n = w.shape[1]
    return pl.pallas_call(
        _norm_matmul_kernel,
        grid=(t // tm, n // tn),
        in_specs=[
            pl.BlockSpec((tm, d), lambda i, j: (i, 0)),
            pl.BlockSpec((1, d), lambda i, j: (0, 0)),
            pl.BlockSpec((d, tn), lambda i, j: (0, j)),
        ],
        out_specs=pl.BlockSpec((tm, tn), lambda i, j: (i, j)),
        out_shape=jax.ShapeDtypeStruct((t, n), BF16),
        scratch_shapes=[pltpu.VMEM((tm, d), BF16)],
        compiler_params=_params("arbitrary", "arbitrary"),
        name="norm_matmul",
    )(x, g.reshape(1, d), w)


def _sigmoid(x):
    return 1.0 / (1.0 + jnp.exp(-x))


def _lru_kernel(xr_ref, gr_ref, cw_ref, cb_ref, gw_ref, gb_ref, lam_ref, o_ref, tail_ref, h_ref, *, tc):
    @pl.when(pl.program_id(1) == 0)
    def _():
        tail_ref[...] = jnp.zeros_like(tail_ref)
        h_ref[...] = jnp.zeros_like(h_ref)

    c = xr_ref.shape[1]
    bw = c // LRU_BLOCKS
    x = xr_ref[...].astype(F32)
    xe = jnp.concatenate([tail_ref[...], x], axis=0)
    cw = cw_ref[...]
    xc = cb_ref[...] + cw[3:4] * x
    for k in range(CONV_WIDTH - 1):
        off = SUBLANES - (CONV_WIDTH - 1 - k)
        xc = xc + cw[k:k + 1] * xe[off:off + tc]
    tail_ref[...] = x[tc - SUBLANES:, :]

    xcb = xc.astype(BF16)
    parts = [jnp.dot(xcb[:, g * bw:(g + 1) * bw], gw_ref[g], preferred_element_type=F32)
             for g in range(LRU_BLOCKS)]
    gb = gb_ref[...]
    r = _sigmoid(jnp.concatenate([p[:, :bw] for p in parts], axis=1) + gb[0:1])
    ig = _sigmoid(jnp.concatenate([p[:, bw:] for p in parts], axis=1) + gb[1:2])

    nlam = -lam_ref[...]
    softplus_nlam = jnp.maximum(nlam, 0.0) + jnp.log1p(jnp.exp(-jnp.abs(nlam)))
    log_a = (-LRU_C) * r * softplus_nlam
    a = jnp.exp(log_a)
    u = jnp.sqrt(1.0 - a * a) * (ig * xc)

    groups = tc // SUBLANES
    a = a.reshape(groups, SUBLANES, c)
    u = u.reshape(groups, SUBLANES, c)
    row_in_group = lax.broadcasted_iota(I32, (groups, SUBLANES, c), 1)
    d = 1
    while d < SUBLANES:
        keep = row_in_group >= d
        a_prev = jnp.where(keep, pltpu.roll(a, d, 1), 1.0)
        u_prev = jnp.where(keep, pltpu.roll(u, d, 1), 0.0)
        u = a * u_prev + u
        a = a * a_prev
        d *= 2
    carry = h_ref[...]
    hs = []
    for g in range(groups):
        hg = u[g] + a[g] * carry
        carry = hg[SUBLANES - 1:SUBLANES, :]
        hs.append(hg)
    h = jnp.concatenate(hs, axis=0)
    h_ref[...] = carry
    o_ref[...] = (h * jax.nn.gelu(gr_ref[...].astype(F32))).astype(o_ref.dtype)


def lru_branch(proj, conv_w, conv_b, gate_w, gate_b, lam, *, batch, seq, width, tc):
    nchunk = seq // tc
    return pl.pallas_call(
        functools.partial(_lru_kernel, tc=tc),
        grid=(batch, nchunk),
        in_specs=[
            pl.BlockSpec((tc, width), lambda b, i: (b * nchunk + i, 0)),
            pl.BlockSpec((tc, width), lambda b, i: (b * nchunk + i, 1)),
            pl.BlockSpec((CONV_WIDTH, width), lambda b, i: (0, 0)),
            pl.BlockSpec((1, width), lambda b, i: (0, 0)),
            pl.BlockSpec(gate_w.shape, lambda b, i: (0, 0, 0)),
            pl.BlockSpec((2, width), lambda b, i: (0, 0)),
            pl.BlockSpec((1, width), lambda b, i: (0, 0)),
        ],
        out_specs=pl.BlockSpec((tc, width), lambda b, i: (b * nchunk + i, 0)),
        out_shape=jax.ShapeDtypeStruct((batch * seq, width), BF16),
        scratch_shapes=[pltpu.VMEM((SUBLANES, width), F32), pltpu.VMEM((1, width), F32)],
        compiler_params=_params("arbitrary", "arbitrary"),
        name="lru",
    )(proj, proj, conv_w, conv_b.reshape(1, width), gate_w, gate_b, lam.reshape(1, width))


def _sba_blocks(qs, kbs, vbs, later, masks, runs, *, scale):
    n = len(qs)
    qb = qs[0].shape[0]
    zs = [lax.dot_general(q, kb, (((1,), (1,)), ((), ())), preferred_element_type=F32) * (scale * LOG2E)
          for q, kb in zip(qs, kbs)]
    softplus = [jnp.maximum(z, 0.0) + jnp.log(1.0 + jnp.exp2(-jnp.abs(z))) * LOG2E for z in zs]
    log_1m = [jnp.where(m, -s, 0.0) for m, s in zip(masks, softplus)]
    his = [x.astype(BF16) for x in log_1m]
    los = [(x - h.astype(F32)).astype(BF16) for x, h in zip(log_1m, his)]
    sums = jnp.dot(jnp.concatenate(his + los, axis=0), later, preferred_element_type=F32)
    outs = []
    for c in range(n):
        rest = sums[c * qb:(c + 1) * qb] + sums[(n + c) * qb:(n + c + 1) * qb]
        log_w = (zs[c] - softplus[c]) + rest
        if runs[c] is not None:
            log_w = log_w + runs[c]
        w = jnp.where(masks[c], jnp.exp2(log_w), 0.0)
        outs.append((w.astype(BF16), rest[:, 0:1] + log_1m[c][:, 0:1]))
    return [(jnp.dot(w, vb, preferred_element_type=F32), total) for (w, total), vb in zip(outs, vbs)]


def _sba_kernel(q_ref, k_ref, v_ref, o_ref, later_ref, acc_ref, run_ref, *, qb, nchain, wk, fk, scale):
    step_id = pl.program_id(2)
    dead_below = F32_EXP_ZERO * LOG2E

    @pl.when(step_id == 0)
    def _():
        r = lax.broadcasted_iota(I32, (wk, wk), 0)
        c = lax.broadcasted_iota(I32, (wk, wk), 1)
        later_ref[...] = jnp.where(r > c, 1.0, 0.0).astype(BF16)

    col_minus_row = (lax.broadcasted_iota(I32, (qb, wk), 1) - lax.broadcasted_iota(I32, (qb, wk), 0))

    def first_key(c):
        q0 = (step_id * nchain + c) * qb
        return q0, pl.multiple_of(jnp.maximum(q0 + qb - wk, 0), qb)

    lefts = []
    masks = []
    for c in range(nchain):
        q0, left = first_key(c)
        lefts.append(left)
        masks.append(col_minus_row < q0 - left)
    first = _sba_blocks([q_ref[c * qb:(c + 1) * qb, :] for c in range(nchain)],
                        [k_ref[pl.ds(left, wk), :] for left in lefts],
                        [v_ref[pl.ds(left, wk), :] for left in lefts],
                        later_ref[...], masks, [None] * nchain, scale=scale)
    worst = first[0][1]
    for c, (out, run) in enumerate(first):
        o_ref[c * qb:(c + 1) * qb, :] = out.astype(o_ref.dtype)
        worst = jnp.maximum(worst, run)

    @pl.when(jnp.max(worst) > dead_below)
    def _():
        for c, (out, run) in enumerate(first):
            acc_ref[c] = out
            run_ref[c] = run
        key_col = lax.broadcasted_iota(I32, (qb, fk), 1)

        def walk_left(c, carry):
            rows = pl.ds(pl.multiple_of(c * qb, qb), qb)

            def cond(state):
                left, live = state
                return jnp.logical_and(left > 0, live)

            def body(state):
                left, _ = state
                start = pl.multiple_of(jnp.maximum(left - fk, 0), qb)
                run = run_ref[c]
                (out, more), = _sba_blocks([q_ref[rows, :]], [k_ref[pl.ds(start, fk), :]],
                                           [v_ref[pl.ds(start, fk), :]], later_ref[0:fk, 0:fk],
                                           [key_col < left - start], [run], scale=scale)
                acc_ref[c] += out
                run_ref[c] = run + more
                return start, jnp.max(run + more) > dead_below

            lax.while_loop(cond, body, (first_key(c)[1], jnp.max(run_ref[c]) > dead_below))
            o_ref[rows, :] = acc_ref[c].astype(o_ref.dtype)
            return carry

        lax.fori_loop(0, nchain, walk_left, 0)


def stick_breaking(proj, *, batch, seq, heads, head_dim, q_col, k_col, v_col, qb, nchain, wk, fk):
    nstep = seq // (qb * nchain)
    rows = qb * nchain
    return pl.pallas_call(
        functools.partial(_sba_kernel, qb=qb, nchain=nchain, wk=wk, fk=fk, scale=head_dim ** -0.5),
        grid=(batch, heads, nstep),
        in_specs=[
            pl.BlockSpec((rows, head_dim), lambda b, h, i: (b * nstep + i, q_col + h)),
            pl.BlockSpec((seq, head_dim), lambda b, h, i: (b, k_col + h)),
            pl.BlockSpec((seq, head_dim), lambda b, h, i: (b, v_col + h)),
        ],
        out_specs=pl.BlockSpec((rows, head_dim), lambda b, h, i: (b * nstep + i, h)),
        out_shape=jax.ShapeDtypeStruct((batch * seq, heads * head_dim), BF16),
        scratch_shapes=[
            pltpu.VMEM((wk, wk), BF16),
            pltpu.VMEM((nchain, qb, head_dim), F32),
            pltpu.VMEM((nchain, qb, 1), F32),
        ],
        compiler_params=_params("arbitrary", "arbitrary", "arbitrary"),
        name="sba",
    )(proj, proj, proj)


def _out_router_kernel(*refs, n_in):
    y_refs = refs[:n_in]
    w_refs = refs[n_in:2 * n_in]
    x_ref, g_ref, rw_ref, rb_ref = refs[2 * n_in:2 * n_in + 4]
    xo_ref, hp_ref, route_ref, cnt_ref, carry_ref = refs[2 * n_in + 4:]

    @pl.when(pl.program_id(0) == 0)
    def _():
        carry_ref[...] = jnp.zeros_like(carry_ref)

    acc = x_ref[...]
    for y_ref, w_ref in zip(y_refs, w_refs):
        acc = acc + jnp.dot(y_ref[...], w_ref[...], preferred_element_type=F32)
    xo_ref[...] = acc

    tm, d = acc.shape
    ms = jnp.mean(acc * acc, axis=-1, keepdims=True)
    h = acc * lax.rsqrt(ms + EPS) * g_ref[...]
    hp_ref[...] = _pack_bf16_pair(h[:, :d // 2], h[:, d // 2:])

    h_hi = h.astype(BF16)
    h_lo = (h - h_hi.astype(F32)).astype(BF16)
    logits = (jnp.dot(h_hi, rw_ref[:, :LANES], preferred_element_type=F32)
              + jnp.dot(h_lo, rw_ref[:, :LANES], preferred_element_type=F32)
              + jnp.dot(h_hi, rw_ref[:, LANES:], preferred_element_type=F32)
              + rb_ref[...])

    lane = lax.broadcasted_iota(I32, (tm, LANES), 1)
    neg = jnp.float32(-jnp.inf)
    big = jnp.int32(LANES)

    def first_argmax(vals, mask):
        m = jnp.max(jnp.where(mask, vals, neg), axis=-1, keepdims=True)
        idx = jnp.min(jnp.where(jnp.logical_and(mask, vals == m), lane, big), axis=-1, keepdims=True)
        return m, idx

    is_group = lane < MOE_GROUPS
    g_max, g_idx = first_argmax(logits, is_group)
    g_w = 1.0 / jnp.sum(jnp.where(is_group, jnp.exp(logits - g_max), 0.0), axis=-1, keepdims=True)
    e_lo = MOE_GROUPS + g_idx * MOE_EXPERTS_PER_GROUP
    in_group = jnp.logical_and(lane >= e_lo, lane < e_lo + MOE_EXPERTS_PER_GROUP)
    l1, i1 = first_argmax(logits, in_group)
    l2, i2 = first_argmax(logits, jnp.logical_and(in_group, lane != i1))
    t = jnp.exp(l2 - l1)
    w1 = g_w / (1.0 + t)
    w2 = g_w * t / (1.0 + t)
    e1 = i1 - MOE_GROUPS
    e2 = i2 - MOE_GROUPS

    onehot = jnp.where(jnp.logical_or(lane == e1, lane == e2), 1.0, 0.0)
    lower = jnp.where(lax.broadcasted_iota(I32, (tm, tm), 1) < lax.broadcasted_iota(I32, (tm, tm), 0),
                      1.0, 0.0).astype(BF16)
    before = jnp.dot(lower, onehot.astype(BF16), preferred_element_type=F32) + carry_ref[...]
    rank1 = jnp.sum(jnp.where(lane == e1, before, 0.0), axis=-1, keepdims=True)
    rank2 = jnp.sum(jnp.where(lane == e2, before, 0.0), axis=-1, keepdims=True)
    new_carry = before[tm - 1:tm, :] + onehot[tm - 1:tm, :]
    carry_ref[...] = new_carry
    cnt_ref[...] = jnp.broadcast_to(new_carry, cnt_ref.shape)

    route = jnp.where(lane == 0, e1.astype(F32), 0.0)
    route = jnp.where(lane == 1, e2.astype(F32), route)
    route = jnp.where(lane == 2, w1, route)
    route = jnp.where(lane == 3, w2, route)
    route = jnp.where(lane == 4, rank1, route)
    route = jnp.where(lane == 5, rank2, route)
    route_ref[...] = route


def out_router(ys, ws, x, g, rw, rb, *, tm):
    t, d = x.shape
    n_in = len(ys)
    in_specs = ([pl.BlockSpec((tm, y.shape[1]), lambda i: (i, 0)) for y in ys]
                + [_resident(w.shape) for w in ws]
                + [pl.BlockSpec((tm, d), lambda i: (i, 0)),
                   _resident((1, d)), _resident(rw.shape), _resident((1, LANES))])
    return pl.pallas_call(
        functools.partial(_out_router_kernel, n_in=n_in),
        grid=(t // tm,),
        in_specs=in_specs,
        out_specs=[
            pl.BlockSpec((tm, d), lambda i: (i, 0)),
            pl.BlockSpec((tm, d // 2), lambda i: (i, 0)),
            pl.BlockSpec((tm, LANES), lambda i: (i, 0)),
            pl.BlockSpec((SUBLANES, LANES), lambda i: (0, 0)),
        ],
        out_shape=[
            jax.ShapeDtypeStruct((t, d), F32),
            jax.ShapeDtypeStruct((t, d // 2), U32),
            jax.ShapeDtypeStruct((t, LANES), F32),
            jax.ShapeDtypeStruct((SUBLANES, LANES), F32),
        ],
        scratch_shapes=[pltpu.VMEM((1, LANES), F32)],
        compiler_params=_params("arbitrary"),
        name="out_router",
    )(*ys, *ws, x, g.reshape(1, d), rw, rb)


def _dispatch_kernel(dest_ref, h_ref, xs_ref, sem, *, tm, top_k):
    base = pl.program_id(0) * tm * top_k

    def row_copy(t, k):
        return pltpu.make_async_copy(h_ref.at[pl.ds(t, 1)],
                                     xs_ref.at[pl.ds(dest_ref[base + t * top_k + k], 1)], sem)

    def issue(t, carry):
        for k in range(top_k):
            row_copy(t, k).start(priority=k % DMA_PRIORITIES)
        return carry

    lax.fori_loop(0, tm, issue, 0, unroll=DMA_ISSUE_UNROLL)

    for k in range(top_k):
        pltpu.make_async_copy(h_ref, xs_ref.at[pl.ds(0, tm)], sem).wait()


def dispatch(dest, h_packed, *, rows_out, tm, top_k):
    t, dw = h_packed.shape
    return pl.pallas_call(
        functools.partial(_dispatch_kernel, tm=tm, top_k=top_k),
        grid_spec=pltpu.PrefetchScalarGridSpec(
            num_scalar_prefetch=1,
            grid=(t // tm,),
            in_specs=[pl.BlockSpec((tm, dw), lambda i, dest: (i, 0))],
            out_specs=pl.BlockSpec(memory_space=pl.ANY),
            scratch_shapes=[pltpu.SemaphoreType.DMA(())],
        ),
        out_shape=jax.ShapeDtypeStruct((rows_out, dw), U32),
        compiler_params=pltpu.CompilerParams(dimension_semantics=("arbitrary",),
                                             vmem_limit_bytes=VMEM_LIMIT, has_side_effects=True),
        name="dispatch",
    )(dest, h_packed)


def _ffn_kernel(blk_e_ref, blk_valid_ref, nused_ref, x_ref, wg_ref, wu_ref, wd_ref, y_ref,
                wg_bf, wu_bf, wd_bf):
    j = pl.program_id(0)

    @pl.when(j < nused_ref[0])
    def _():
        @pl.when(jnp.logical_or(j == 0, blk_e_ref[j] != blk_e_ref[jnp.maximum(j - 1, 0)]))
        def _():
            wg_bf[...] = wg_ref[0, 0].astype(BF16)
            wu_bf[...] = wu_ref[0, 0].astype(BF16)
            wd_bf[...] = wd_ref[0, 0].astype(BF16)

        rows = lax.broadcasted_iota(I32, x_ref.shape, 0)
        packed = jnp.where(rows < blk_valid_ref[j], x_ref[...], jnp.uint32(0))
        a, b = _unpack_bf16_pair(packed)
        x = jnp.concatenate([a.astype(BF16), b.astype(BF16)], axis=1)
        gate = jnp.dot(x, wg_bf[...], preferred_element_type=F32)
        up = jnp.dot(x, wu_bf[...], preferred_element_type=F32)
        hid = (gate * _sigmoid(gate) * up).astype(BF16)
        y = jnp.dot(hid, wd_bf[...], preferred_element_type=F32)
        half = y.shape[1] // 2
        y_ref[...] = _pack_bf16_pair(y[:, :half], y[:, half:])


def expert_ffn(blk_e, blk_valid, nused, xs, w_gate, w_up, w_down, *, layer, tb):
    p, dw = xs.shape
    _, _, d, ff = w_gate.shape

    def row_map(j, be, bv, nu):
        return (jnp.minimum(j, nu[0] - 1), 0)

    def w_map(j, be, bv, nu):
        return (layer, be[j], 0, 0)

    return pl.pallas_call(
        _ffn_kernel,
        grid_spec=pltpu.PrefetchScalarGridSpec(
            num_scalar_prefetch=3,
            grid=(p // tb,),
            in_specs=[
                pl.BlockSpec((tb, dw), row_map),
                pl.BlockSpec((1, 1, d, ff), w_map),
                pl.BlockSpec((1, 1, d, ff), w_map),
                pl.BlockSpec((1, 1, ff, d), w_map),
            ],
            out_specs=pl.BlockSpec((tb, dw), row_map),
            scratch_shapes=[pltpu.VMEM((d, ff), BF16), pltpu.VMEM((d, ff), BF16), pltpu.VMEM((ff, d), BF16)],
        ),
        out_shape=jax.ShapeDtypeStruct((p, dw), U32),
        compiler_params=_params("arbitrary"),
        name="ffn",
    )(blk_e, blk_valid, nused, xs, w_gate, w_up, w_down)


def _combine_kernel(dest_ref, x_ref, route_ref, g_ref, ys_ref, o_ref, buf_ref, sem, *, tm, top_k, final_norm):
    i = pl.program_id(0)
    slot = i % 2

    def gather_tile(tile, slot):
        base = tile * tm * top_k

        def issue(t, carry):
            for k in range(top_k):
                pltpu.make_async_copy(ys_ref.at[pl.ds(dest_ref[base + t * top_k + k], 1)],
                                      buf_ref.at[slot, k, pl.ds(t, 1)], sem.at[slot]).start(
                                          priority=k % DMA_PRIORITIES)
            return carry

        lax.fori_loop(0, tm, issue, 0, unroll=DMA_ISSUE_UNROLL)

    @pl.when(i == 0)
    def _():
        gather_tile(i, slot)

    @pl.when(i + 1 < pl.num_programs(0))
    def _():
        gather_tile(i + 1, 1 - slot)

    for k in range(top_k):
        pltpu.make_async_copy(ys_ref.at[pl.ds(0, tm)], buf_ref.at[slot, k], sem.at[slot]).wait()

    route = route_ref[...]
    out = x_ref[...]
    for k in range(top_k):
        a, b = _unpack_bf16_pair(buf_ref[slot, k])
        out = out + route[:, 2 + k:3 + k] * jnp.concatenate([a, b], axis=1)
    if final_norm:
        ms = jnp.mean(out * out, axis=-1, keepdims=True)
        out = out * lax.rsqrt(ms + EPS) * g_ref[...]
    o_ref[...] = out


def combine(dest, x, route, g, ys, *, tm, top_k, final_norm):
    t, d = x.shape
    dw = ys.shape[1]
    return pl.pallas_call(
        functools.partial(_combine_kernel, tm=tm, top_k=top_k, final_norm=final_norm),
        grid_spec=pltpu.PrefetchScalarGridSpec(
            num_scalar_prefetch=1,
            grid=(t // tm,),
            in_specs=[
                pl.BlockSpec((tm, d), lambda i, dest: (i, 0)),
                pl.BlockSpec((tm, LANES), lambda i, dest: (i, 0)),
                pl.BlockSpec((1, d), lambda i, dest: (0, 0)),
                pl.BlockSpec(memory_space=pl.ANY),
            ],
            out_specs=pl.BlockSpec((tm, d), lambda i, dest: (i, 0)),
            scratch_shapes=[pltpu.VMEM((2, top_k, tm, dw), U32), pltpu.SemaphoreType.DMA((2,))],
        ),
        out_shape=jax.ShapeDtypeStruct((t, d), F32),
        compiler_params=_params("arbitrary"),
        name="combine",
    )(dest, x, route, g.reshape(1, d), ys)


def moe_layer(x_new, h_packed, route, counts, w_gate, w_up, w_down, g_final, *, layer, tb, tm_dispatch, tm_combine,
              final_norm):
    t = x_new.shape[0]
    top_k = 2
    cnt = counts[0, :MOE_EXPERTS].astype(I32)
    padded = ((cnt + tb - 1) // tb) * tb
    pends = jnp.cumsum(padded)
    pstarts = pends - padded
    expert_ids = jnp.arange(MOE_EXPERTS, dtype=I32)
    experts = route[:, 0:top_k].astype(I32)
    ranks = route[:, 4:4 + top_k].astype(I32)
    start_of = jnp.sum(jnp.where(experts[..., None] == expert_ids, pstarts, 0), axis=-1)
    dest = (start_of + ranks).reshape(t * top_k)
    rows_out = t * top_k + MOE_EXPERTS * tb
    nb = rows_out // tb
    blk_start = jnp.arange(nb, dtype=I32) * tb
    blk_e = jnp.minimum(jnp.sum((blk_start[:, None] >= pends[None, :]).astype(I32), axis=1), MOE_EXPERTS - 1)
    blk_onehot = blk_e[:, None] == expert_ids
    blk_end = jnp.sum(jnp.where(blk_onehot, pstarts + cnt, 0), axis=1)
    blk_valid = jnp.clip(blk_end - blk_start, 0, tb).astype(I32)
    nused = (pends[-1:] // tb).astype(I32)

    xs = dispatch(dest, h_packed, rows_out=rows_out, tm=tm_dispatch, top_k=top_k)
    ys = expert_ffn(blk_e, blk_valid, nused, xs, w_gate, w_up, w_down, layer=layer, tb=tb)
    return combine(dest, x_new, route, g_final, ys, tm=tm_combine, top_k=top_k, final_norm=final_norm)


def _rope_kernel(pos_ref, inv_ref, cos_ref, sin_ref):
    ang = pos_ref[...].astype(F32) * inv_ref[...]
    cos_ref[...] = jnp.cos(ang)
    sin_ref[...] = jnp.sin(ang)


def rope_table(positions, half, *, tm):
    t = positions.size
    inv_freq = (ROPE_BASE ** (-jnp.arange(half, dtype=F32) / half)).reshape(1, half)
    return pl.pallas_call(
        _rope_kernel,
        grid=(t // tm,),
        in_specs=[pl.BlockSpec((tm, 1), lambda i: (i, 0)), pl.BlockSpec((1, half), lambda i: (0, 0))],
        out_specs=[pl.BlockSpec((tm, half), lambda i: (i, 0))] * 2,
        out_shape=[jax.ShapeDtypeStruct((t, half), F32)] * 2,
        compiler_params=_params("arbitrary"),
        name="rope_table",
    )(positions.reshape(t, 1), inv_freq)


def _retention_kernel(lg_ref, q_ref, k_ref, v_ref, g_ref, cos_ref, sin_ref, gain_ref, o_ref,
                      state_ref, rhs_ref, kscale_ref, eps_ref, *, c, hp):
    dk = q_ref.shape[1] // hp
    dv = v_ref.shape[1] // hp
    half = dk // 2
    heads = range(hp)
    lgs = [lg_ref[pl.program_id(1) * hp + h] for h in heads]
    pos = lax.broadcasted_iota(I32, (c, dk), 0).astype(F32)

    @pl.when(pl.program_id(2) == 0)
    def _():
        state_ref[...] = jnp.zeros_like(state_ref)
        for h in heads:
            rhs_ref[h, c:, :] = jnp.zeros((dk, dv), BF16)
            kscale_ref[h] = jnp.exp(-(pos + 1.0) * lgs[h]) * (dk ** -0.5)
            eps_ref[h] = jnp.exp(-2.0 * (pos[:, 0:1] + 1.0) * lgs[h]) * (dv * EPS)

    cos = cos_ref[...]
    sin = sin_ref[...]

    def rotate(x):
        x1, x2 = x[:, :half], x[:, half:]
        return jnp.concatenate([x1 * cos - x2 * sin, x2 * cos + x1 * sin], axis=1)

    causal = lax.broadcasted_iota(I32, (c, c), 0) >= lax.broadcasted_iota(I32, (c, c), 1)

    def scores(h):
        qr = rotate(q_ref[:, h * dk:(h + 1) * dk].astype(F32)).astype(BF16)
        kp = (rotate(k_ref[:, h * dk:(h + 1) * dk].astype(F32)) * kscale_ref[h]).astype(BF16)
        s = lax.dot_general(qr, kp, (((1,), (1,)), ((), ())), preferred_element_type=F32)
        return qr, kp, s

    ahead = scores(0)
    for h in heads:
        qr, kp, s = ahead
        if h + 1 < hp:
            ahead = scores(h + 1)
        v = v_ref[:, h * dv:(h + 1) * dv]
        rhs_ref[h, :c, :] = v
        kv = lax.dot_general(kp, v, (((0,), (0,)), ((), ())), preferred_element_type=F32)
        s = jnp.where(causal, s, 0.0).astype(BF16)
        y = jnp.dot(jnp.concatenate([s, qr], axis=1), rhs_ref[h], preferred_element_type=F32)

        state = (state_ref[h] + kv) * jnp.exp(jnp.zeros((1, dv), F32) + c * lgs[h])
        state_ref[h] = state
        rhs_ref[h, c:, :] = state.astype(BF16)

        yc = y - jnp.sum(y, axis=-1, keepdims=True) * (1.0 / dv)
        ss = jnp.sum(yc * yc, axis=-1, keepdims=True)
        yn = yc * lax.rsqrt(ss + eps_ref[h]) * (gain_ref[0, h:h + 1, :] * (dv ** 0.5))
        g = g_ref[:, h * dv:(h + 1) * dv]
        gate = g / (1.0 + jnp.exp(-g))
        o_ref[:, h * dv:(h + 1) * dv] = yn.astype(o_ref.dtype) * gate


def retention(proj, cos, sin, gain, *, batch, seq, heads, dk, dv, c, hp):
    nchunk = seq // c
    log_g = jnp.log1p(-jnp.power(2.0, -5.0 - jnp.arange(heads, dtype=F32)))
    groups = heads // hp
    k_col = groups
    v_col = 2 * heads * dk // (hp * dv)
    g_col = v_col + groups
    half = dk // 2

    def rows(b, n):
        return b * nchunk + n

    return pl.pallas_call(
        functools.partial(_retention_kernel, c=c, hp=hp),
        grid_spec=pltpu.PrefetchScalarGridSpec(
            num_scalar_prefetch=1,
            grid=(batch, groups, nchunk),
            in_specs=[
                pl.BlockSpec((c, hp * dk), lambda b, h, n, lg: (rows(b, n), h)),
                pl.BlockSpec((c, hp * dk), lambda b, h, n, lg: (rows(b, n), k_col + h)),
                pl.BlockSpec((c, hp * dv), lambda b, h, n, lg: (rows(b, n), v_col + h)),
                pl.BlockSpec((c, hp * dv), lambda b, h, n, lg: (rows(b, n), g_col + h)),
                pl.BlockSpec((c, half), lambda b, h, n, lg: (rows(b, n), 0)),
                pl.BlockSpec((c, half), lambda b, h, n, lg: (rows(b, n), 0)),
                pl.BlockSpec((1, hp, dv), lambda b, h, n, lg: (h, 0, 0)),
            ],
            out_specs=pl.BlockSpec((c, hp * dv), lambda b, h, n, lg: (rows(b, n), h)),
            scratch_shapes=[
                pltpu.VMEM((hp, dk, dv), F32),
                pltpu.VMEM((hp, c + dk, dv), BF16),
                pltpu.VMEM((hp, c, dk), F32),
                pltpu.VMEM((hp, c, 1), F32),
            ],
        ),
        out_shape=jax.ShapeDtypeStruct((batch * seq, heads * dv), BF16),
        compiler_params=_params("arbitrary", "arbitrary", "arbitrary"),
        name="retention",
    )(log_g, proj, proj, proj, proj, cos, sin, gain.reshape(groups, hp, dv))


def _router_params(wg, bg, we, be):
    d = wg.shape[0]
    n = MOE_GROUPS + MOE_EXPERTS
    w = jnp.concatenate([wg, we, jnp.zeros((d, LANES - n), F32)], axis=1)
    b = jnp.concatenate([bg, be, jnp.zeros((LANES - n,), F32)]).reshape(1, LANES)
    w_hi = w.astype(BF16)
    w_lo = (w - w_hi.astype(F32)).astype(BF16)
    return jnp.concatenate([w_hi, w_lo], axis=1), b


def _pick(total, want):
    t = min(want, total)
    while total % t:
        t //= 2
    return t


def kernel(x, positions, ev_norm, ev_w_in, ev_conv_w, ev_conv_b, ev_gate_a_w, ev_gate_a_b, ev_gate_x_w, ev_gate_x_b, ev_lru_lambda, ev_w_out, od_norm, od_w_in, od_gn_gain, od_w_out, ffn_norm, router_group_w, router_group_b, router_expert_w, router_expert_b, expert_w_gate, expert_w_up, expert_w_down, final_norm):
    batch, seq, d = x.shape
    t = batch * seq
    depth = ffn_norm.shape[0]
    lru_w = ev_conv_w.shape[2]
    sb_w = (ev_w_in.shape[2] - 2 * lru_w) // 3
    head_dim = sb_w // SB_HEADS
    dv = od_gn_gain.shape[2]
    dk = (od_w_in.shape[2] - 2 * RET_HEADS * dv) // (2 * RET_HEADS)

    tm_proj = _pick(t, 1024)
    tm_out = _pick(t, 512)
    tm_dispatch = _pick(t, 2048)
    tm_combine = _pick(t, 1024)
    tb = 512

    xf = x.reshape(t, d)
    for layer in range(depth):
        i = layer // 2
        if layer % 2 == 0:
            proj = norm_matmul(xf, ev_norm[i], ev_w_in[i].astype(BF16), tm=tm_proj, tn=_pick(ev_w_in.shape[2], 1024))
            gate_w = jnp.concatenate([ev_gate_a_w[i], ev_gate_x_w[i]], axis=2).astype(BF16)
            gate_b = jnp.stack([ev_gate_a_b[i].reshape(lru_w), ev_gate_x_b[i].reshape(lru_w)])
            y_a = lru_branch(proj, ev_conv_w[i], ev_conv_b[i], gate_w, gate_b, ev_lru_lambda[i],
                             batch=batch, seq=seq, width=lru_w, tc=_pick(seq, 512))
            col0 = 2 * lru_w // head_dim
            y_b = stick_breaking(proj, batch=batch, seq=seq, heads=SB_HEADS, head_dim=head_dim,
                                 q_col=col0, k_col=col0 + SB_HEADS, v_col=col0 + 2 * SB_HEADS,
                                 qb=SBA_QUERY_BLOCK, nchain=_pick(seq // SBA_QUERY_BLOCK, SBA_CHAINS),
                                 wk=SBA_FIRST_KEYS, fk=SBA_MORE_KEYS)
            w_out = ev_w_out[i].astype(BF16)
            ys, ws = [y_a, y_b], [w_out[:lru_w], w_out[lru_w:]]
        else:
            proj = norm_matmul(xf, od_norm[i], od_w_in[i].astype(BF16), tm=tm_proj, tn=_pick(od_w_in.shape[2], 2048))
            cos, sin = rope_table(positions, dk // 2, tm=_pick(t, 1024))
            y = retention(proj, cos, sin, od_gn_gain[i], batch=batch, seq=seq, heads=RET_HEADS,
                          dk=dk, dv=dv, c=_pick(seq, 256), hp=RET_HEADS_PER_STEP)
            ys, ws = [y], [od_w_out[i].astype(BF16)]

        rw, rb = _router_params(router_group_w[layer], router_group_b[layer],
                                router_expert_w[layer], router_expert_b[layer])
        x_new, h_packed, route, counts = out_router(ys, ws, xf, ffn_norm[layer], rw, rb, tm=tm_out)
        last = layer == depth - 1
        xf = moe_layer(x_new, h_packed, route, counts, expert_w_gate, expert_w_up, expert_w_down, final_norm,
                       layer=layer, tb=tb, tm_dispatch=tm_dispatch, tm_combine=tm_combine, final_norm=last)
    return xf.reshape(batch, seq, d)
```

```python
import functools

import jax
import jax.numpy as jnp
from jax import lax
from jax.experimental import pallas as pl
from jax.experimental.pallas import tpu as pltpu

F32 = jnp.float32
BF16 = jnp.bfloat16
U32 = jnp.uint32
I32 = jnp.int32

LANES = 128
SUBLANES = 8
VMEM_BYTES_V7X = 64 * 1024 * 1024
VMEM_LIMIT = VMEM_BYTES_V7X * 7 // 8

EPS = 1e-6
LRU_BLOCKS = 8
CONV_WIDTH = 4
LRU_C = 8.0
SB_HEADS = 8
RET_HEADS = 8
ROPE_BASE = 10000.0
MOE_GROUPS = 4
MOE_EXPERTS_PER_GROUP = 8
MOE_EXPERTS = MOE_GROUPS * MOE_EXPERTS_PER_GROUP

SBA_QUERY_BLOCK = 64
SBA_CHAINS = 32
SBA_FIRST_KEYS = 256
SBA_MORE_KEYS = 128
LOG2E = 1.4426950408889634

RET_HEADS_PER_STEP = 8

DMA_ISSUE_UNROLL = 8

F32_EXP_ZERO = -104.0

HI16 = 0xFFFF0000


def _params(*sem):
    return pltpu.CompilerParams(dimension_semantics=sem, vmem_limit_bytes=VMEM_LIMIT)


def _resident(shape):
    return pl.BlockSpec(shape, lambda *_: (0,) * len(shape), pipeline_mode=pl.Buffered(1))


def _pack_bf16_pair(a, b):
    ua = lax.bitcast_convert_type(a.astype(BF16).astype(F32), U32)
    ub = lax.bitcast_convert_type(b.astype(BF16).astype(F32), U32)
    return (ua >> 16) | (ub & jnp.uint32(HI16))


def _unpack_bf16_pair(w):
    a = lax.bitcast_convert_type(w << 16, F32)
    b = lax.bitcast_convert_type(w & jnp.uint32(HI16), F32)
    return a, b


def _norm_matmul_kernel(x_ref, g_ref, w_ref, o_ref, h_ref):
    @pl.when(pl.program_id(1) == 0)
    def _():
        x = x_ref[...]
        ms = jnp.mean(x * x, axis=-1, keepdims=True)
        h_ref[...] = (x * lax.rsqrt(ms + EPS) * g_ref[...]).astype(h_ref.dtype)

    o_ref[...] = jnp.dot(h_ref[...], w_ref[...], preferred_element_type=F32).astype(o_ref.dtype)


def norm_matmul(x, g, w, *, tm, tn):
    t, d = x.shape
    n = w.shape[1]
    w_spec = _resident((d, n)) if tn == n else pl.BlockSpec((d, tn), lambda i, j: (0, j))
    return pl.pallas_call(
        _norm_matmul_kernel,
        grid=(t // tm, n // tn),
        in_specs=[
            pl.BlockSpec((tm, d), lambda i, j: (i, 0)),
            pl.BlockSpec((1, d), lambda i, j: (0, 0)),
            w_spec,
        ],
        out_specs=pl.BlockSpec((tm, tn), lambda i, j: (i, j)),
        out_shape=jax.ShapeDtypeStruct((t, n), BF16),
        scratch_shapes=[pltpu.VMEM((tm, d), BF16)],
        compiler_params=_params("arbitrary", "arbitrary"),
        name="norm_matmul",
    )(x, g.reshape(1, d), w)


def _sigmoid(x):
    return 1.0 / (1.0 + jnp.exp(-x))


def _lru_kernel(xr_ref, gr_ref, cw_ref, cb_ref, gw_ref, gb_ref, lam_ref, o_ref, tail_ref, h_ref, *, tc):
    @pl.when(pl.program_id(1) == 0)
    def _():
        tail_ref[...] = jnp.zeros_like(tail_ref)
        h_ref[...] = jnp.zeros_like(h_ref)

    c = xr_ref.shape[1]
    bw = c // LRU_BLOCKS
    x = xr_ref[...].astype(F32)
    xe = jnp.concatenate([tail_ref[...], x], axis=0)
    cw = cw_ref[...]
    xc = cb_ref[...] + cw[3:4] * x
    for k in range(CONV_WIDTH - 1):
        off = SUBLANES - (CONV_WIDTH - 1 - k)
        xc = xc + cw[k:k + 1] * xe[off:off + tc]
    tail_ref[...] = x[tc - SUBLANES:, :]

    xcb = xc.astype(BF16)
    parts = [jnp.dot(xcb[:, g * bw:(g + 1) * bw], gw_ref[g], preferred_element_type=F32)
             for g in range(LRU_BLOCKS)]
    gb = gb_ref[...]
    r = _sigmoid(jnp.concatenate([p[:, :bw] for p in parts], axis=1) + gb[0:1])
    ig = _sigmoid(jnp.concatenate([p[:, bw:] for p in parts], axis=1) + gb[1:2])

    nlam = -lam_ref[...]
    softplus_nlam = jnp.maximum(nlam, 0.0) + jnp.log1p(jnp.exp(-jnp.abs(nlam)))
    log_a = (-LRU_C) * r * softplus_nlam
    a = jnp.exp(log_a)
    u = jnp.sqrt(1.0 - a * a) * (ig * xc)

    groups = tc // SUBLANES
    a = a.reshape(groups, SUBLANES, c)
    u = u.reshape(groups, SUBLANES, c)
    row_in_group = lax.broadcasted_iota(I32, (groups, SUBLANES, c), 1)
    d = 1
    while d < SUBLANES:
        keep = row_in_group >= d
        a_prev = jnp.where(keep, pltpu.roll(a, d, 1), 1.0)
        u_prev = jnp.where(keep, pltpu.roll(u, d, 1), 0.0)
        u = a * u_prev + u
        a = a * a_prev
        d *= 2
    carry = h_ref[...]
    hs = []
    for g in range(groups):
        hg = u[g] + a[g] * carry
        carry = hg[SUBLANES - 1:SUBLANES, :]
        hs.append(hg)
    h = jnp.concatenate(hs, axis=0)
    h_ref[...] = carry
    o_ref[...] = (h * jax.nn.gelu(gr_ref[...].astype(F32))).astype(o_ref.dtype)


def lru_branch(proj, conv_w, conv_b, gate_w, gate_b, lam, *, batch, seq, width, tc):
    nchunk = seq // tc
    return pl.pallas_call(
        functools.partial(_lru_kernel, tc=tc),
        grid=(batch, nchunk),
        in_specs=[
            pl.BlockSpec((tc, width), lambda b, i: (b * nchunk + i, 0)),
            pl.BlockSpec((tc, width), lambda b, i: (b * nchunk + i, 1)),
            pl.BlockSpec((CONV_WIDTH, width), lambda b, i: (0, 0)),
            pl.BlockSpec((1, width), lambda b, i: (0, 0)),
            pl.BlockSpec(gate_w.shape, lambda b, i: (0, 0, 0)),
            pl.BlockSpec((2, width), lambda b, i: (0, 0)),
            pl.BlockSpec((1, width), lambda b, i: (0, 0)),
        ],
        out_specs=pl.BlockSpec((tc, width), lambda b, i: (b * nchunk + i, 0)),
        out_shape=jax.ShapeDtypeStruct((batch * seq, width), BF16),
        scratch_shapes=[pltpu.VMEM((SUBLANES, width), F32), pltpu.VMEM((1, width), F32)],
        compiler_params=_params("arbitrary", "arbitrary"),
        name="lru",
    )(proj, proj, conv_w, conv_b.reshape(1, width), gate_w, gate_b, lam.reshape(1, width))


def _sba_blocks(qs, kbs, vbs, later, masks, runs, *, scale):
    n = len(qs)
    qb = qs[0].shape[0]
    zs = [lax.dot_general(q, kb, (((1,), (1,)), ((), ())), preferred_element_type=F32) * (scale * LOG2E)
          for q, kb in zip(qs, kbs)]
    softplus = [jnp.maximum(z, 0.0) + jnp.log(1.0 + jnp.exp2(-jnp.abs(z))) * LOG2E for z in zs]
    log_1m = [jnp.where(m, -s, 0.0) for m, s in zip(masks, softplus)]
    his = [x.astype(BF16) for x in log_1m]
    los = [(x - h.astype(F32)).astype(BF16) for x, h in zip(log_1m, his)]
    sums = jnp.dot(jnp.concatenate(his + los, axis=0), later, preferred_element_type=F32)
    outs = []
    for c in range(n):
        rest = sums[c * qb:(c + 1) * qb] + sums[(n + c) * qb:(n + c + 1) * qb]
        log_w = (zs[c] - softplus[c]) + rest
        if runs[c] is not None:
            log_w = log_w + runs[c]
        w = jnp.where(masks[c], jnp.exp2(log_w), 0.0)
        outs.append((w.astype(BF16), rest[:, 0:1] + log_1m[c][:, 0:1]))
    return [(jnp.dot(w, vb, preferred_element_type=F32), total) for (w, total), vb in zip(outs, vbs)]


def _sba_kernel(q_ref, k_ref, v_ref, o_ref, later_ref, acc_ref, run_ref, *, qb, nchain, wk, fk, scale):
    step_id = pl.program_id(2)
    dead_below = F32_EXP_ZERO * LOG2E

    @pl.when(step_id == 0)
    def _():
        r = lax.broadcasted_iota(I32, (wk, wk), 0)
        c = lax.broadcasted_iota(I32, (wk, wk), 1)
        later_ref[...] = jnp.where(r > c, 1.0, 0.0).astype(BF16)

    col_minus_row = (lax.broadcasted_iota(I32, (qb, wk), 1) - lax.broadcasted_iota(I32, (qb, wk), 0))

    def first_key(c):
        q0 = (step_id * nchain + c) * qb
        return q0, pl.multiple_of(jnp.maximum(q0 + qb - wk, 0), qb)

    lefts = []
    masks = []
    for c in range(nchain):
        q0, left = first_key(c)
        lefts.append(left)
        masks.append(col_minus_row < q0 - left)
    first = _sba_blocks([q_ref[c * qb:(c + 1) * qb, :] for c in range(nchain)],
                        [k_ref[pl.ds(left, wk), :] for left in lefts],
                        [v_ref[pl.ds(left, wk), :] for left in lefts],
                        later_ref[...], masks, [None] * nchain, scale=scale)
    worst = first[0][1]
    for c, (out, run) in enumerate(first):
        o_ref[c * qb:(c + 1) * qb, :] = out.astype(o_ref.dtype)
        worst = jnp.maximum(worst, run)

    @pl.when(jnp.max(worst) > dead_below)
    def _():
        for c, (out, run) in enumerate(first):
            acc_ref[c] = out
            run_ref[c] = run
        key_col = lax.broadcasted_iota(I32, (qb, fk), 1)

        def walk_left(c, carry):
            rows = pl.ds(pl.multiple_of(c * qb, qb), qb)

            def cond(state):
                left, live = state
                return jnp.logical_and(left > 0, live)

            def body(state):
                left, _ = state
                start = pl.multiple_of(jnp.maximum(left - fk, 0), qb)
                run = run_ref[c]
                (out, more), = _sba_blocks([q_ref[rows, :]], [k_ref[pl.ds(start, fk), :]],
                                           [v_ref[pl.ds(start, fk), :]], later_ref[0:fk, 0:fk],
                                           [key_col < left - start], [run], scale=scale)
                acc_ref[c] += out
                run_ref[c] = run + more
                return start, jnp.max(run + more) > dead_below

            lax.while_loop(cond, body, (first_key(c)[1], jnp.max(run_ref[c]) > dead_below))
            o_ref[rows, :] = acc_ref[c].astype(o_ref.dtype)
            return carry

        lax.fori_loop(0, nchain, walk_left, 0)


def stick_breaking(proj, *, batch, seq, heads, head_dim, q_col, k_col, v_col, qb, nchain, wk, fk):
    nstep = seq // (qb * nchain)
    rows = qb * nchain
    return pl.pallas_call(
        functools.partial(_sba_kernel, qb=qb, nchain=nchain, wk=wk, fk=fk, scale=head_dim ** -0.5),
        grid=(batch, heads, nstep),
        in_specs=[
            pl.BlockSpec((rows, head_dim), lambda b, h, i: (b * nstep + i, q_col + h)),
            pl.BlockSpec((seq, head_dim), lambda b, h, i: (b, k_col + h)),
            pl.BlockSpec((seq, head_dim), lambda b, h, i: (b, v_col + h)),
        ],
        out_specs=pl.BlockSpec((rows, head_dim), lambda b, h, i: (b * nstep + i, h)),
        out_shape=jax.ShapeDtypeStruct((batch * seq, heads * head_dim), BF16),
        scratch_shapes=[
            pltpu.VMEM((wk, wk), BF16),
            pltpu.VMEM((nchain, qb, head_dim), F32),
            pltpu.VMEM((nchain, qb, 1), F32),
        ],
        compiler_params=_params("arbitrary", "arbitrary", "arbitrary"),
        name="sba",
    )(proj, proj, proj)


def _out_router_kernel(*refs, n_in):
    y_refs = refs[:n_in]
    w_refs = refs[n_in:2 * n_in]
    x_ref, g_ref, rw_ref, rb_ref = refs[2 * n_in:2 * n_in + 4]
    xo_ref, hp_ref, route_ref, cnt_ref, carry_ref = refs[2 * n_in + 4:]

    @pl.when(pl.program_id(0) == 0)
    def _():
        carry_ref[...] = jnp.zeros_like(carry_ref)

    acc = x_ref[...]
    for y_ref, w_ref in zip(y_refs, w_refs):
        acc = acc + jnp.dot(y_ref[...], w_ref[...], preferred_element_type=F32)
    xo_ref[...] = acc

    tm, d = acc.shape
    ms = jnp.mean(acc * acc, axis=-1, keepdims=True)
    h = acc * lax.rsqrt(ms + EPS) * g_ref[...]
    hp_ref[...] = _pack_bf16_pair(h[:, :d // 2], h[:, d // 2:])

    h_hi = h.astype(BF16)
    h_lo = (h - h_hi.astype(F32)).astype(BF16)
    logits = (jnp.dot(h_hi, rw_ref[:, :LANES], preferred_element_type=F32)
              + jnp.dot(h_lo, rw_ref[:, :LANES], preferred_element_type=F32)
              + jnp.dot(h_hi, rw_ref[:, LANES:], preferred_element_type=F32)
              + rb_ref[...])

    lane = lax.broadcasted_iota(I32, (tm, LANES), 1)
    neg = jnp.float32(-jnp.inf)
    big = jnp.int32(LANES)

    def first_argmax(vals, mask):
        m = jnp.max(jnp.where(mask, vals, neg), axis=-1, keepdims=True)
        idx = jnp.min(jnp.where(jnp.logical_and(mask, vals == m), lane, big), axis=-1, keepdims=True)
        return m, idx

    is_group = lane < MOE_GROUPS
    g_max, g_idx = first_argmax(logits, is_group)
    g_w = 1.0 / jnp.sum(jnp.where(is_group, jnp.exp(logits - g_max), 0.0), axis=-1, keepdims=True)
    e_lo = MOE_GROUPS + g_idx * MOE_EXPERTS_PER_GROUP
    in_group = jnp.logical_and(lane >= e_lo, lane < e_lo + MOE_EXPERTS_PER_GROUP)
    l1, i1 = first_argmax(logits, in_group)
    l2, i2 = first_argmax(logits, jnp.logical_and(in_group, lane != i1))
    t = jnp.exp(l2 - l1)
    w1 = g_w / (1.0 + t)
    w2 = g_w * t / (1.0 + t)
    e1 = i1 - MOE_GROUPS
    e2 = i2 - MOE_GROUPS

    onehot = jnp.where(jnp.logical_or(lane == e1, lane == e2), 1.0, 0.0)
    lower = jnp.where(lax.broadcasted_iota(I32, (tm, tm), 1) < lax.broadcasted_iota(I32, (tm, tm), 0),
                      1.0, 0.0).astype(BF16)
    before = jnp.dot(lower, onehot.astype(BF16), preferred_element_type=F32) + carry_ref[...]
    rank1 = jnp.sum(jnp.where(lane == e1, before, 0.0), axis=-1, keepdims=True)
    rank2 = jnp.sum(jnp.where(lane == e2, before, 0.0), axis=-1, keepdims=True)
    new_carry = before[tm - 1:tm, :] + onehot[tm - 1:tm, :]
    carry_ref[...] = new_carry
    cnt_ref[...] = jnp.broadcast_to(new_carry, cnt_ref.shape)

    route = jnp.where(lane == 0, e1.astype(F32), 0.0)
    route = jnp.where(lane == 1, e2.astype(F32), route)
    route = jnp.where(lane == 2, w1, route)
    route = jnp.where(lane == 3, w2, route)
    route = jnp.where(lane == 4, rank1, route)
    route = jnp.where(lane == 5, rank2, route)
    route_ref[...] = route


def out_router(ys, ws, x, g, rw, rb, *, tm):
    t, d = x.shape
    n_in = len(ys)
    in_specs = ([pl.BlockSpec((tm, y.shape[1]), lambda i: (i, 0)) for y in ys]
                + [_resident(w.shape) for w in ws]
                + [pl.BlockSpec((tm, d), lambda i: (i, 0)),
                   _resident((1, d)), _resident(rw.shape), _resident((1, LANES))])
    return pl.pallas_call(
        functools.partial(_out_router_kernel, n_in=n_in),
        grid=(t // tm,),
        in_specs=in_specs,
        out_specs=[
            pl.BlockSpec((tm, d), lambda i: (i, 0)),
            pl.BlockSpec((tm, d // 2), lambda i: (i, 0)),
            pl.BlockSpec((tm, LANES), lambda i: (i, 0)),
            pl.BlockSpec((SUBLANES, LANES), lambda i: (0, 0)),
        ],
        out_shape=[
            jax.ShapeDtypeStruct((t, d), F32),
            jax.ShapeDtypeStruct((t, d // 2), U32),
            jax.ShapeDtypeStruct((t, LANES), F32),
            jax.ShapeDtypeStruct((SUBLANES, LANES), F32),
        ],
        scratch_shapes=[pltpu.VMEM((1, LANES), F32)],
        compiler_params=_params("arbitrary"),
        name="out_router",
    )(*ys, *ws, x, g.reshape(1, d), rw, rb)


def _dispatch_kernel(dest_ref, h_ref, xs_ref, sem, *, tm, top_k):
    base = pl.program_id(0) * tm * top_k

    def row_copy(t, k):
        return pltpu.make_async_copy(h_ref.at[pl.ds(t, 1)],
                                     xs_ref.at[pl.ds(dest_ref[base + t * top_k + k], 1)], sem)

    def issue(t, carry):
        for k in range(top_k):
            row_copy(t, k).start()
        return carry

    lax.fori_loop(0, tm, issue, 0, unroll=DMA_ISSUE_UNROLL)

    for k in range(top_k):
        pltpu.make_async_copy(h_ref, xs_ref.at[pl.ds(0, tm)], sem).wait()


def dispatch(dest, h_packed, *, rows_out, tm, top_k):
    t, dw = h_packed.shape
    return pl.pallas_call(
        functools.partial(_dispatch_kernel, tm=tm, top_k=top_k),
        grid_spec=pltpu.PrefetchScalarGridSpec(
            num_scalar_prefetch=1,
            grid=(t // tm,),
            in_specs=[pl.BlockSpec((tm, dw), lambda i, dest: (i, 0))],
            out_specs=pl.BlockSpec(memory_space=pl.ANY),
            scratch_shapes=[pltpu.SemaphoreType.DMA(())],
        ),
        out_shape=jax.ShapeDtypeStruct((rows_out, dw), U32),
        compiler_params=pltpu.CompilerParams(dimension_semantics=("arbitrary",),
                                             vmem_limit_bytes=VMEM_LIMIT, has_side_effects=True),
        name="dispatch",
    )(dest, h_packed)


def _ffn_kernel(blk_e_ref, blk_valid_ref, nused_ref, x_ref, wg_ref, wu_ref, wd_ref, y_ref,
                wg_bf, wu_bf, wd_bf):
    j = pl.program_id(0)

    @pl.when(j < nused_ref[0])
    def _():
        @pl.when(jnp.logical_or(j == 0, blk_e_ref[j] != blk_e_ref[jnp.maximum(j - 1, 0)]))
        def _():
            wg_bf[...] = wg_ref[0, 0].astype(BF16)
            wu_bf[...] = wu_ref[0, 0].astype(BF16)
            wd_bf[...] = wd_ref[0, 0].astype(BF16)

        rows = lax.broadcasted_iota(I32, x_ref.shape, 0)
        packed = jnp.where(rows < blk_valid_ref[j], x_ref[...], jnp.uint32(0))
        a, b = _unpack_bf16_pair(packed)
        x = jnp.concatenate([a.astype(BF16), b.astype(BF16)], axis=1)
        gate = jnp.dot(x, wg_bf[...], preferred_element_type=F32)
        up = jnp.dot(x, wu_bf[...], preferred_element_type=F32)
        hid = (gate * _sigmoid(gate) * up).astype(BF16)
        y = jnp.dot(hid, wd_bf[...], preferred_element_type=F32)
        half = y.shape[1] // 2
        y_ref[...] = _pack_bf16_pair(y[:, :half], y[:, half:])


def expert_ffn(blk_e, blk_valid, nused, xs, w_gate, w_up, w_down, *, layer, tb):
    p, dw = xs.shape
    _, _, d, ff = w_gate.shape

    def row_map(j, be, bv, nu):
        return (jnp.minimum(j, nu[0] - 1), 0)

    def w_map(j, be, bv, nu):
        return (layer, be[j], 0, 0)

    return pl.pallas_call(
        _ffn_kernel,
        grid_spec=pltpu.PrefetchScalarGridSpec(
            num_scalar_prefetch=3,
            grid=(p // tb,),
            in_specs=[
                pl.BlockSpec((tb, dw), row_map),
                pl.BlockSpec((1, 1, d, ff), w_map),
                pl.BlockSpec((1, 1, d, ff), w_map),
                pl.BlockSpec((1, 1, ff, d), w_map),
            ],
            out_specs=pl.BlockSpec((tb, dw), row_map),
            scratch_shapes=[pltpu.VMEM((d, ff), BF16), pltpu.VMEM((d, ff), BF16), pltpu.VMEM((ff, d), BF16)],
        ),
        out_shape=jax.ShapeDtypeStruct((p, dw), U32),
        compiler_params=_params("arbitrary"),
        name="ffn",
    )(blk_e, blk_valid, nused, xs, w_gate, w_up, w_down)


def _combine_kernel(dest_ref, x_ref, route_ref, g_ref, ys_ref, o_ref, buf_ref, sem, *, tm, top_k, final_norm):
    i = pl.program_id(0)
    slot = i % 2

    def gather_tile(tile, slot):
        base = tile * tm * top_k

        def issue(t, carry):
            for k in range(top_k):
                pltpu.make_async_copy(ys_ref.at[pl.ds(dest_ref[base + t * top_k + k], 1)],
                                      buf_ref.at[slot, k, pl.ds(t, 1)], sem.at[slot]).start()
            return carry

        lax.fori_loop(0, tm, issue, 0, unroll=DMA_ISSUE_UNROLL)

    @pl.when(i == 0)
    def _():
        gather_tile(i, slot)

    @pl.when(i + 1 < pl.num_programs(0))
    def _():
        gather_tile(i + 1, 1 - slot)

    for k in range(top_k):
        pltpu.make_async_copy(ys_ref.at[pl.ds(0, tm)], buf_ref.at[slot, k], sem.at[slot]).wait()

    route = route_ref[...]
    out = x_ref[...]
    for k in range(top_k):
        a, b = _unpack_bf16_pair(buf_ref[slot, k])
        out = out + route[:, 2 + k:3 + k] * jnp.concatenate([a, b], axis=1)
    if final_norm:
        ms = jnp.mean(out * out, axis=-1, keepdims=True)
        out = out * lax.rsqrt(ms + EPS) * g_ref[...]
    o_ref[...] = out


def combine(dest, x, route, g, ys, *, tm, top_k, final_norm):
    t, d = x.shape
    dw = ys.shape[1]
    return pl.pallas_call(
        functools.partial(_combine_kernel, tm=tm, top_k=top_k, final_norm=final_norm),
        grid_spec=pltpu.PrefetchScalarGridSpec(
            num_scalar_prefetch=1,
            grid=(t // tm,),
            in_specs=[
                pl.BlockSpec((tm, d), lambda i, dest: (i, 0)),
                pl.BlockSpec((tm, LANES), lambda i, dest: (i, 0)),
                pl.BlockSpec((1, d), lambda i, dest: (0, 0)),
                pl.BlockSpec(memory_space=pl.ANY),
            ],
            out_specs=pl.BlockSpec((tm, d), lambda i, dest: (i, 0)),
            scratch_shapes=[pltpu.VMEM((2, top_k, tm, dw), U32), pltpu.SemaphoreType.DMA((2,))],
        ),
        out_shape=jax.ShapeDtypeStruct((t, d), F32),
        compiler_params=_params("arbitrary"),
        name="combine",
    )(dest, x, route, g.reshape(1, d), ys)


def moe_layer(x_new, h_packed, route, counts, w_gate, w_up, w_down, g_final, *, layer, tb, tm_dispatch, tm_combine,
              final_norm):
    t = x_new.shape[0]
    top_k = 2
    cnt = counts[0, :MOE_EXPERTS].astype(I32)
    padded = ((cnt + tb - 1) // tb) * tb
    pends = jnp.cumsum(padded)
    pstarts = pends - padded
    expert_ids = jnp.arange(MOE_EXPERTS, dtype=I32)
    experts = route[:, 0:top_k].astype(I32)
    ranks = route[:, 4:4 + top_k].astype(I32)
    start_of = jnp.sum(jnp.where(experts[..., None] == expert_ids, pstarts, 0), axis=-1)
    dest = (start_of + ranks).reshape(t * top_k)
    rows_out = t * top_k + MOE_EXPERTS * tb
    nb = rows_out // tb
    blk_start = jnp.arange(nb, dtype=I32) * tb
    blk_e = jnp.minimum(jnp.sum((blk_start[:, None] >= pends[None, :]).astype(I32), axis=1), MOE_EXPERTS - 1)
    blk_onehot = blk_e[:, None] == expert_ids
    blk_end = jnp.sum(jnp.where(blk_onehot, pstarts + cnt, 0), axis=1)
    blk_valid = jnp.clip(blk_end - blk_start, 0, tb).astype(I32)
    nused = (pends[-1:] // tb).astype(I32)

    xs = dispatch(dest, h_packed, rows_out=rows_out, tm=tm_dispatch, top_k=top_k)
    ys = expert_ffn(blk_e, blk_valid, nused, xs, w_gate, w_up, w_down, layer=layer, tb=tb)
    return combine(dest, x_new, route, g_final, ys, tm=tm_combine, top_k=top_k, final_norm=final_norm)


def _rope_kernel(pos_ref, inv_ref, cos_ref, sin_ref):
    ang = pos_ref[...].astype(F32) * inv_ref[...]
    cos_ref[...] = jnp.cos(ang)
    sin_ref[...] = jnp.sin(ang)


def rope_table(positions, half, *, tm):
    t = positions.size
    inv_freq = (ROPE_BASE ** (-jnp.arange(half, dtype=F32) / half)).reshape(1, half)
    return pl.pallas_call(
        _rope_kernel,
        grid=(t // tm,),
        in_specs=[pl.BlockSpec((tm, 1), lambda i: (i, 0)), pl.BlockSpec((1, half), lambda i: (0, 0))],
        out_specs=[pl.BlockSpec((tm, half), lambda i: (i, 0))] * 2,
        out_shape=[jax.ShapeDtypeStruct((t, half), F32)] * 2,
        compiler_params=_params("arbitrary"),
        name="rope_table",
    )(positions.reshape(t, 1), inv_freq)


def _retention_kernel(lg_ref, q_ref, k_ref, v_ref, g_ref, cos_ref, sin_ref, gain_ref, o_ref,
                      state_ref, rhs_ref, kscale_ref, eps_ref, *, c, hp):
    dk = q_ref.shape[1] // hp
    dv = v_ref.shape[1] // hp
    half = dk // 2
    heads = range(hp)
    lgs = [lg_ref[pl.program_id(1) * hp + h] for h in heads]
    pos = lax.broadcasted_iota(I32, (c, dk), 0).astype(F32)

    @pl.when(pl.program_id(2) == 0)
    def _():
        state_ref[...] = jnp.zeros_like(state_ref)
        for h in heads:
            rhs_ref[h, c:, :] = jnp.zeros((dk, dv), BF16)
            kscale_ref[h] = jnp.exp(-(pos + 1.0) * lgs[h]) * (dk ** -0.5)
            eps_ref[h] = jnp.exp(-2.0 * (pos[:, 0:1] + 1.0) * lgs[h]) * (dv * EPS)

    cos = cos_ref[...]
    sin = sin_ref[...]

    def rotate(x):
        x1, x2 = x[:, :half], x[:, half:]
        return jnp.concatenate([x1 * cos - x2 * sin, x2 * cos + x1 * sin], axis=1)

    causal = lax.broadcasted_iota(I32, (c, c), 0) >= lax.broadcasted_iota(I32, (c, c), 1)

    def scores(h):
        qr = rotate(q_ref[:, h * dk:(h + 1) * dk].astype(F32)).astype(BF16)
        kp = (rotate(k_ref[:, h * dk:(h + 1) * dk].astype(F32)) * kscale_ref[h]).astype(BF16)
        s = lax.dot_general(qr, kp, (((1,), (1,)), ((), ())), preferred_element_type=F32)
        return qr, kp, s

    ahead = scores(0)
    for h in heads:
        qr, kp, s = ahead
        if h + 1 < hp:
            ahead = scores(h + 1)
        v = v_ref[:, h * dv:(h + 1) * dv]
        rhs_ref[h, :c, :] = v
        kv = lax.dot_general(kp, v, (((0,), (0,)), ((), ())), preferred_element_type=F32)
        s = jnp.where(causal, s, 0.0).astype(BF16)
        y = jnp.dot(jnp.concatenate([s, qr], axis=1), rhs_ref[h], preferred_element_type=F32)

        state = (state_ref[h] + kv) * jnp.exp(jnp.zeros((1, dv), F32) + c * lgs[h])
        state_ref[h] = state
        rhs_ref[h, c:, :] = state.astype(BF16)

        yc = y - jnp.sum(y, axis=-1, keepdims=True) * (1.0 / dv)
        ss = jnp.sum(yc * yc, axis=-1, keepdims=True)
        yn = yc * lax.rsqrt(ss + eps_ref[h]) * (gain_ref[0, h:h + 1, :] * (dv ** 0.5))
        g = g_ref[:, h * dv:(h + 1) * dv]
        gate = g / (1.0 + jnp.exp(-g))
        o_ref[:, h * dv:(h + 1) * dv] = yn.astype(o_ref.dtype) * gate


def retention(proj, cos, sin, gain, *, batch, seq, heads, dk, dv, c, hp):
    nchunk = seq // c
    log_g = jnp.log1p(-jnp.power(2.0, -5.0 - jnp.arange(heads, dtype=F32)))
    groups = heads // hp
    k_col = groups
    v_col = 2 * heads * dk // (hp * dv)
    g_col = v_col + groups
    half = dk // 2

    def rows(b, n):
        return b * nchunk + n

    return pl.pallas_call(
        functools.partial(_retention_kernel, c=c, hp=hp),
        grid_spec=pltpu.PrefetchScalarGridSpec(
            num_scalar_prefetch=1,
            grid=(batch, groups, nchunk),
            in_specs=[
                pl.BlockSpec((c, hp * dk), lambda b, h, n, lg: (rows(b, n), h)),
                pl.BlockSpec((c, hp * dk), lambda b, h, n, lg: (rows(b, n), k_col + h)),
                pl.BlockSpec((c, hp * dv), lambda b, h, n, lg: (rows(b, n), v_col + h)),
                pl.BlockSpec((c, hp * dv), lambda b, h, n, lg: (rows(b, n), g_col + h)),
                pl.BlockSpec((c, half), lambda b, h, n, lg: (rows(b, n), 0)),
                pl.BlockSpec((c, half), lambda b, h, n, lg: (rows(b, n), 0)),
                pl.BlockSpec((1, hp, dv), lambda b, h, n, lg: (h, 0, 0)),
            ],
            out_specs=pl.BlockSpec((c, hp * dv), lambda b, h, n, lg: (rows(b, n), h)),
            scratch_shapes=[
                pltpu.VMEM((hp, dk, dv), F32),
                pltpu.VMEM((hp, c + dk, dv), BF16),
                pltpu.VMEM((hp, c, dk), F32),
                pltpu.VMEM((hp, c, 1), F32),
            ],
        ),
        out_shape=jax.ShapeDtypeStruct((batch * seq, heads * dv), BF16),
        compiler_params=_params("arbitrary", "arbitrary", "arbitrary"),
        name="retention",
    )(log_g, proj, proj, proj, proj, cos, sin, gain.reshape(groups, hp, dv))


def _router_params(wg, bg, we, be):
    d = wg.shape[0]
    n = MOE_GROUPS + MOE_EXPERTS
    w = jnp.concatenate([wg, we, jnp.zeros((d, LANES - n), F32)], axis=1)
    b = jnp.concatenate([bg, be, jnp.zeros((LANES - n,), F32)]).reshape(1, LANES)
    w_hi = w.astype(BF16)
    w_lo = (w - w_hi.astype(F32)).astype(BF16)
    return jnp.concatenate([w_hi, w_lo], axis=1), b


def _pick(total, want):
    t = min(want, total)
    while total % t:
        t //= 2
    return t


def kernel(x, positions, ev_norm, ev_w_in, ev_conv_w, ev_conv_b, ev_gate_a_w, ev_gate_a_b, ev_gate_x_w, ev_gate_x_b, ev_lru_lambda, ev_w_out, od_norm, od_w_in, od_gn_gain, od_w_out, ffn_norm, router_group_w, router_group_b, router_expert_w, router_expert_b, expert_w_gate, expert_w_up, expert_w_down, final_norm):
    batch, seq, d = x.shape
    t = batch * seq
    depth = ffn_norm.shape[0]
    lru_w = ev_conv_w.shape[2]
    sb_w = (ev_w_in.shape[2] - 2 * lru_w) // 3
    head_dim = sb_w // SB_HEADS
    dv = od_gn_gain.shape[2]
    dk = (od_w_in.shape[2] - 2 * RET_HEADS * dv) // (2 * RET_HEADS)

    tm_proj = _pick(t, 1024)
    tm_out = _pick(t, 512)
    tm_dispatch = _pick(t, 2048)
    tm_combine = _pick(t, 512)
    tb = 512

    xf = x.reshape(t, d)
    for layer in range(depth):
        i = layer // 2
        if layer % 2 == 0:
            proj = norm_matmul(xf, ev_norm[i], ev_w_in[i].astype(BF16), tm=_pick(t, 512), tn=ev_w_in.shape[2])
            gate_w = jnp.concatenate([ev_gate_a_w[i], ev_gate_x_w[i]], axis=2).astype(BF16)
            gate_b = jnp.stack([ev_gate_a_b[i].reshape(lru_w), ev_gate_x_b[i].reshape(lru_w)])
            y_a = lru_branch(proj, ev_conv_w[i], ev_conv_b[i], gate_w, gate_b, ev_lru_lambda[i],
                             batch=batch, seq=seq, width=lru_w, tc=_pick(seq, 512))
            col0 = 2 * lru_w // head_dim
            y_b = stick_breaking(proj, batch=batch, seq=seq, heads=SB_HEADS, head_dim=head_dim,
                                 q_col=col0, k_col=col0 + SB_HEADS, v_col=col0 + 2 * SB_HEADS,
                                 qb=SBA_QUERY_BLOCK, nchain=_pick(seq // SBA_QUERY_BLOCK, SBA_CHAINS),
                                 wk=SBA_FIRST_KEYS, fk=SBA_MORE_KEYS)
            w_out = ev_w_out[i].astype(BF16)
            ys, ws = [y_a, y_b], [w_out[:lru_w], w_out[lru_w:]]
        else:
            proj = norm_matmul(xf, od_norm[i], od_w_in[i].astype(BF16), tm=tm_proj, tn=_pick(od_w_in.shape[2], 2048))
            cos, sin = rope_table(positions, dk // 2, tm=_pick(t, 1024))
            y = retention(proj, cos, sin, od_gn_gain[i], batch=batch, seq=seq, heads=RET_HEADS,
                          dk=dk, dv=dv, c=_pick(seq, 256), hp=RET_HEADS_PER_STEP)
            ys, ws = [y], [od_w_out[i].astype(BF16)]

        rw, rb = _router_params(router_group_w[layer], router_group_b[layer],
                                router_expert_w[layer], router_expert_b[layer])
        x_new, h_packed, route, counts = out_router(ys, ws, xf, ffn_norm[layer], rw, rb, tm=tm_out)
        last = layer == depth - 1
        xf = moe_layer(x_new, h_packed, route, counts, expert_w_gate, expert_w_up, expert_w_down, final_norm,
                       layer=layer, tb=tb, tm_dispatch=tm_dispatch, tm_combine=tm_combine, final_norm=last)
    return xf.reshape(batch, seq, d)
```

```python
import functools

import jax
import jax.numpy as jnp
from jax import lax
from jax.experimental import pallas as pl
from jax.experimental.pallas import tpu as pltpu

F32 = jnp.float32
BF16 = jnp.bfloat16
U32 = jnp.uint32
I32 = jnp.int32

LANES = 128
SUBLANES = 8
VMEM_BYTES_V7X = 64 * 1024 * 1024
VMEM_LIMIT = VMEM_BYTES_V7X * 7 // 8

EPS = 1e-6
LRU_BLOCKS = 8
CONV_WIDTH = 4
LRU_C = 8.0
SB_HEADS = 8
RET_HEADS = 8
ROPE_BASE = 10000.0
MOE_GROUPS = 4
MOE_EXPERTS_PER_GROUP = 8
MOE_EXPERTS = MOE_GROUPS * MOE_EXPERTS_PER_GROUP

SBA_QUERY_BLOCK = 64
SBA_CHAINS = 64
SBA_FIRST_KEYS = 256
SBA_MORE_KEYS = 128
LOG2E = 1.4426950408889634

RET_HEADS_PER_STEP = 8

DMA_ISSUE_UNROLL = 8

F32_EXP_ZERO = -104.0

HI16 = 0xFFFF0000


def _params(*sem):
    return pltpu.CompilerParams(dimension_semantics=sem, vmem_limit_bytes=VMEM_LIMIT)


def _resident(shape):
    return pl.BlockSpec(shape, lambda *_: (0,) * len(shape), pipeline_mode=pl.Buffered(1))


def _pack_bf16_pair(a, b):
    ua = lax.bitcast_convert_type(a.astype(BF16).astype(F32), U32)
    ub = lax.bitcast_convert_type(b.astype(BF16).astype(F32), U32)
    return (ua >> 16) | (ub & jnp.uint32(HI16))


def _unpack_bf16_pair(w):
    a = lax.bitcast_convert_type(w << 16, F32)
    b = lax.bitcast_convert_type(w & jnp.uint32(HI16), F32)
    return a, b


def _norm_matmul_kernel(x_ref, g_ref, w_ref, o_ref, h_ref):
    @pl.when(pl.program_id(1) == 0)
    def _():
        x = x_ref[...]
        ms = jnp.mean(x * x, axis=-1, keepdims=True)
        h_ref[...] = (x * lax.rsqrt(ms + EPS) * g_ref[...]).astype(h_ref.dtype)

    o_ref[...] = jnp.dot(h_ref[...], w_ref[...], preferred_element_type=F32).astype(o_ref.dtype)


def norm_matmul(x, g, w, *, tm, tn):
    t, d = x.shape
    n = w.shape[1]
    w_spec = _resident((d, n)) if tn == n else pl.BlockSpec((d, tn), lambda i, j: (0, j))
    return pl.pallas_call(
        _norm_matmul_kernel,
        grid=(t // tm, n // tn),
        in_specs=[
            pl.BlockSpec((tm, d), lambda i, j: (i, 0)),
            pl.BlockSpec((1, d), lambda i, j: (0, 0)),
            w_spec,
        ],
        out_specs=pl.BlockSpec((tm, tn), lambda i, j: (i, j)),
        out_shape=jax.ShapeDtypeStruct((t, n), BF16),
        scratch_shapes=[pltpu.VMEM((tm, d), BF16)],
        compiler_params=_params("arbitrary", "arbitrary"),
        name="norm_matmul",
    )(x, g.reshape(1, d), w)


def _sigmoid(x):
    return 1.0 / (1.0 + jnp.exp(-x))


def _lru_kernel(xr_ref, gr_ref, cw_ref, cb_ref, gw_ref, gb_ref, lam_ref, o_ref, tail_ref, h_ref, *, tc):
    @pl.when(pl.program_id(1) == 0)
    def _():
        tail_ref[...] = jnp.zeros_like(tail_ref)
        h_ref[...] = jnp.zeros_like(h_ref)

    c = xr_ref.shape[1]
    bw = c // LRU_BLOCKS
    x = xr_ref[...].astype(F32)
    xe = jnp.concatenate([tail_ref[...], x], axis=0)
    cw = cw_ref[...]
    xc = cb_ref[...] + cw[3:4] * x
    for k in range(CONV_WIDTH - 1):
        off = SUBLANES - (CONV_WIDTH - 1 - k)
        xc = xc + cw[k:k + 1] * xe[off:off + tc]
    tail_ref[...] = x[tc - SUBLANES:, :]

    xcb = xc.astype(BF16)
    parts = [jnp.dot(xcb[:, g * bw:(g + 1) * bw], gw_ref[g], preferred_element_type=F32)
             for g in range(LRU_BLOCKS)]
    gb = gb_ref[...]
    r = _sigmoid(jnp.concatenate([p[:, :bw] for p in parts], axis=1) + gb[0:1])
    ig = _sigmoid(jnp.concatenate([p[:, bw:] for p in parts], axis=1) + gb[1:2])

    nlam = -lam_ref[...]
    softplus_nlam = jnp.maximum(nlam, 0.0) + jnp.log1p(jnp.exp(-jnp.abs(nlam)))
    log_a = (-LRU_C) * r * softplus_nlam
    a = jnp.exp(log_a)
    u = jnp.sqrt(1.0 - a * a) * (ig * xc)

    groups = tc // SUBLANES
    a = a.reshape(groups, SUBLANES, c)
    u = u.reshape(groups, SUBLANES, c)
    row_in_group = lax.broadcasted_iota(I32, (groups, SUBLANES, c), 1)
    d = 1
    while d < SUBLANES:
        keep = row_in_group >= d
        a_prev = jnp.where(keep, pltpu.roll(a, d, 1), 1.0)
        u_prev = jnp.where(keep, pltpu.roll(u, d, 1), 0.0)
        u = a * u_prev + u
        a = a * a_prev
        d *= 2
    carry = h_ref[...]
    hs = []
    for g in range(groups):
        hg = u[g] + a[g] * carry
        carry = hg[SUBLANES - 1:SUBLANES, :]
        hs.append(hg)
    h = jnp.concatenate(hs, axis=0)
    h_ref[...] = carry
    o_ref[...] = (h * jax.nn.gelu(gr_ref[...].astype(F32))).astype(o_ref.dtype)


def lru_branch(proj, conv_w, conv_b, gate_w, gate_b, lam, *, batch, seq, width, tc):
    nchunk = seq // tc
    return pl.pallas_call(
        functools.partial(_lru_kernel, tc=tc),
        grid=(batch, nchunk),
        in_specs=[
            pl.BlockSpec((tc, width), lambda b, i: (b * nchunk + i, 0)),
            pl.BlockSpec((tc, width), lambda b, i: (b * nchunk + i, 1)),
            pl.BlockSpec((CONV_WIDTH, width), lambda b, i: (0, 0)),
            pl.BlockSpec((1, width), lambda b, i: (0, 0)),
            pl.BlockSpec(gate_w.shape, lambda b, i: (0, 0, 0)),
            pl.BlockSpec((2, width), lambda b, i: (0, 0)),
            pl.BlockSpec((1, width), lambda b, i: (0, 0)),
        ],
        out_specs=pl.BlockSpec((tc, width), lambda b, i: (b * nchunk + i, 0)),
        out_shape=jax.ShapeDtypeStruct((batch * seq, width), BF16),
        scratch_shapes=[pltpu.VMEM((SUBLANES, width), F32), pltpu.VMEM((1, width), F32)],
        compiler_params=_params("arbitrary", "arbitrary"),
        name="lru",
    )(proj, proj, conv_w, conv_b.reshape(1, width), gate_w, gate_b, lam.reshape(1, width))


def _sba_blocks(qs, kbs, vbs, later, masks, runs, *, scale):
    n = len(qs)
    qb = qs[0].shape[0]
    zs = [lax.dot_general(q, kb, (((1,), (1,)), ((), ())), preferred_element_type=F32) * (scale * LOG2E)
          for q, kb in zip(qs, kbs)]
    softplus = [jnp.maximum(z, 0.0) + jnp.log(1.0 + jnp.exp2(-jnp.abs(z))) * LOG2E for z in zs]
    log_1m = [jnp.where(m, -s, 0.0) for m, s in zip(masks, softplus)]
    his = [x.astype(BF16) for x in log_1m]
    los = [(x - h.astype(F32)).astype(BF16) for x, h in zip(log_1m, his)]
    sums = jnp.dot(jnp.concatenate(his + los, axis=0), later, preferred_element_type=F32)
    outs = []
    for c in range(n):
        rest = sums[c * qb:(c + 1) * qb] + sums[(n + c) * qb:(n + c + 1) * qb]
        log_w = (zs[c] - softplus[c]) + rest
        if runs[c] is not None:
            log_w = log_w + runs[c]
        w = jnp.where(masks[c], jnp.exp2(log_w), 0.0)
        outs.append((w.astype(BF16), rest[:, 0:1] + log_1m[c][:, 0:1]))
    return [(jnp.dot(w, vb, preferred_element_type=F32), total) for (w, total), vb in zip(outs, vbs)]


def _sba_kernel(q_ref, k_ref, v_ref, o_ref, later_ref, acc_ref, run_ref, *, qb, nchain, wk, fk, scale):
    step_id = pl.program_id(2)
    dead_below = F32_EXP_ZERO * LOG2E

    @pl.when(step_id == 0)
    def _():
        r = lax.broadcasted_iota(I32, (wk, wk), 0)
        c = lax.broadcasted_iota(I32, (wk, wk), 1)
        later_ref[...] = jnp.where(r > c, 1.0, 0.0).astype(BF16)

    col_minus_row = (lax.broadcasted_iota(I32, (qb, wk), 1) - lax.broadcasted_iota(I32, (qb, wk), 0))

    def first_key(c):
        q0 = (step_id * nchain + c) * qb
        return q0, pl.multiple_of(jnp.maximum(q0 + qb - wk, 0), qb)

    lefts = []
    masks = []
    for c in range(nchain):
        q0, left = first_key(c)
        lefts.append(left)
        masks.append(col_minus_row < q0 - left)
    first = _sba_blocks([q_ref[c * qb:(c + 1) * qb, :] for c in range(nchain)],
                        [k_ref[pl.ds(left, wk), :] for left in lefts],
                        [v_ref[pl.ds(left, wk), :] for left in lefts],
                        later_ref[...], masks, [None] * nchain, scale=scale)
    worst = first[0][1]
    for c, (out, run) in enumerate(first):
        o_ref[c * qb:(c + 1) * qb, :] = out.astype(o_ref.dtype)
        worst = jnp.maximum(worst, run)

    @pl.when(jnp.max(worst) > dead_below)
    def _():
        for c, (out, run) in enumerate(first):
            acc_ref[c] = out
            run_ref[c] = run
        key_col = lax.broadcasted_iota(I32, (qb, fk), 1)

        def walk_left(c, carry):
            rows = pl.ds(pl.multiple_of(c * qb, qb), qb)

            def cond(state):
                left, live = state
                return jnp.logical_and(left > 0, live)

            def body(state):
                left, _ = state
                start = pl.multiple_of(jnp.maximum(left - fk, 0), qb)
                run = run_ref[c]
                (out, more), = _sba_blocks([q_ref[rows, :]], [k_ref[pl.ds(start, fk), :]],
                                           [v_ref[pl.ds(start, fk), :]], later_ref[0:fk, 0:fk],
                                           [key_col < left - start], [run], scale=scale)
                acc_ref[c] += out
                run_ref[c] = run + more
                return start, jnp.max(run + more) > dead_below

            lax.while_loop(cond, body, (first_key(c)[1], jnp.max(run_ref[c]) > dead_below))
            o_ref[rows, :] = acc_ref[c].astype(o_ref.dtype)
            return carry

        lax.fori_loop(0, nchain, walk_left, 0)


def stick_breaking(proj, *, batch, seq, heads, head_dim, q_col, k_col, v_col, qb, nchain, wk, fk):
    nstep = seq // (qb * nchain)
    rows = qb * nchain
    return pl.pallas_call(
        functools.partial(_sba_kernel, qb=qb, nchain=nchain, wk=wk, fk=fk, scale=head_dim ** -0.5),
        grid=(batch, heads, nstep),
        in_specs=[
            pl.BlockSpec((rows, head_dim), lambda b, h, i: (b * nstep + i, q_col + h)),
            pl.BlockSpec((seq, head_dim), lambda b, h, i: (b, k_col + h)),
            pl.BlockSpec((seq, head_dim), lambda b, h, i: (b, v_col + h)),
        ],
        out_specs=pl.BlockSpec((rows, head_dim), lambda b, h, i: (b * nstep + i, h)),
        out_shape=jax.ShapeDtypeStruct((batch * seq, heads * head_dim), BF16),
        scratch_shapes=[
            pltpu.VMEM((wk, wk), BF16),
            pltpu.VMEM((nchain, qb, head_dim), F32),
            pltpu.VMEM((nchain, qb, 1), F32),
        ],
        compiler_params=_params("arbitrary", "arbitrary", "arbitrary"),
        name="sba",
    )(proj, proj, proj)


def _out_router_kernel(*refs, n_in):
    y_refs = refs[:n_in]
    w_refs = refs[n_in:2 * n_in]
    x_ref, g_ref, rw_ref, rb_ref = refs[2 * n_in:2 * n_in + 4]
    xo_ref, hp_ref, route_ref, cnt_ref, carry_ref = refs[2 * n_in + 4:]

    @pl.when(pl.program_id(0) == 0)
    def _():
        carry_ref[...] = jnp.zeros_like(carry_ref)

    acc = x_ref[...]
    for y_ref, w_ref in zip(y_refs, w_refs):
        acc = acc + jnp.dot(y_ref[...], w_ref[...], preferred_element_type=F32)
    xo_ref[...] = acc

    tm, d = acc.shape
    ms = jnp.mean(acc * acc, axis=-1, keepdims=True)
    h = acc * lax.rsqrt(ms + EPS) * g_ref[...]
    hp_ref[...] = _pack_bf16_pair(h[:, :d // 2], h[:, d // 2:])

    h_hi = h.astype(BF16)
    h_lo = (h - h_hi.astype(F32)).astype(BF16)
    logits = (jnp.dot(h_hi, rw_ref[:, :LANES], preferred_element_type=F32)
              + jnp.dot(h_lo, rw_ref[:, :LANES], preferred_element_type=F32)
              + jnp.dot(h_hi, rw_ref[:, LANES:], preferred_element_type=F32)
              + rb_ref[...])

    lane = lax.broadcasted_iota(I32, (tm, LANES), 1)
    neg = jnp.float32(-jnp.inf)
    big = jnp.int32(LANES)

    def first_argmax(vals, mask):
        m = jnp.max(jnp.where(mask, vals, neg), axis=-1, keepdims=True)
        idx = jnp.min(jnp.where(jnp.logical_and(mask, vals == m), lane, big), axis=-1, keepdims=True)
        return m, idx

    is_group = lane < MOE_GROUPS
    g_max, g_idx = first_argmax(logits, is_group)
    g_w = 1.0 / jnp.sum(jnp.where(is_group, jnp.exp(logits - g_max), 0.0), axis=-1, keepdims=True)
    e_lo = MOE_GROUPS + g_idx * MOE_EXPERTS_PER_GROUP
    in_group = jnp.logical_and(lane >= e_lo, lane < e_lo + MOE_EXPERTS_PER_GROUP)
    l1, i1 = first_argmax(logits, in_group)
    l2, i2 = first_argmax(logits, jnp.logical_and(in_group, lane != i1))
    t = jnp.exp(l2 - l1)
    w1 = g_w / (1.0 + t)
    w2 = g_w * t / (1.0 + t)
    e1 = i1 - MOE_GROUPS
    e2 = i2 - MOE_GROUPS

    onehot = jnp.where(jnp.logical_or(lane == e1, lane == e2), 1.0, 0.0)
    lower = jnp.where(lax.broadcasted_iota(I32, (tm, tm), 1) < lax.broadcasted_iota(I32, (tm, tm), 0),
                      1.0, 0.0).astype(BF16)
    before = jnp.dot(lower, onehot.astype(BF16), preferred_element_type=F32) + carry_ref[...]
    rank1 = jnp.sum(jnp.where(lane == e1, before, 0.0), axis=-1, keepdims=True)
    rank2 = jnp.sum(jnp.where(lane == e2, before, 0.0), axis=-1, keepdims=True)
    new_carry = before[tm - 1:tm, :] + onehot[tm - 1:tm, :]
    carry_ref[...] = new_carry
    cnt_ref[...] = jnp.broadcast_to(new_carry, cnt_ref.shape)

    route = jnp.where(lane == 0, e1.astype(F32), 0.0)
    route = jnp.where(lane == 1, e2.astype(F32), route)
    route = jnp.where(lane == 2, w1, route)
    route = jnp.where(lane == 3, w2, route)
    route = jnp.where(lane == 4, rank1, route)
    route = jnp.where(lane == 5, rank2, route)
    route_ref[...] = route


def out_router(ys, ws, x, g, rw, rb, *, tm):
    t, d = x.shape
    n_in = len(ys)
    in_specs = ([pl.BlockSpec((tm, y.shape[1]), lambda i: (i, 0)) for y in ys]
                + [_resident(w.shape) for w in ws]
                + [pl.BlockSpec((tm, d), lambda i: (i, 0)),
                   _resident((1, d)), _resident(rw.shape), _resident((1, LANES))])
    return pl.pallas_call(
        functools.partial(_out_router_kernel, n_in=n_in),
        grid=(t // tm,),
        in_specs=in_specs,
        out_specs=[
            pl.BlockSpec((tm, d), lambda i: (i, 0)),
            pl.BlockSpec((tm, d // 2), lambda i: (i, 0)),
            pl.BlockSpec((tm, LANES), lambda i: (i, 0)),
            pl.BlockSpec((SUBLANES, LANES), lambda i: (0, 0)),
        ],
        out_shape=[
            jax.ShapeDtypeStruct((t, d), F32),
            jax.ShapeDtypeStruct((t, d // 2), U32),
            jax.ShapeDtypeStruct((t, LANES), F32),
            jax.ShapeDtypeStruct((SUBLANES, LANES), F32),
        ],
        scratch_shapes=[pltpu.VMEM((1, LANES), F32)],
        compiler_params=_params("arbitrary"),
        name="out_router",
    )(*ys, *ws, x, g.reshape(1, d), rw, rb)


def _dispatch_kernel(dest_ref, h_ref, xs_ref, sem, *, tm, top_k):
    base = pl.program_id(0) * tm * top_k

    def row_copy(t, k):
        return pltpu.make_async_copy(h_ref.at[pl.ds(t, 1)],
                                     xs_ref.at[pl.ds(dest_ref[base + t * top_k + k], 1)], sem)

    def issue(t, carry):
        for k in range(top_k):
            row_copy(t, k).start()
        return carry

    lax.fori_loop(0, tm, issue, 0, unroll=DMA_ISSUE_UNROLL)

    for k in range(top_k):
        pltpu.make_async_copy(h_ref, xs_ref.at[pl.ds(0, tm)], sem).wait()


def dispatch(dest, h_packed, *, rows_out, tm, top_k):
    t, dw = h_packed.shape
    return pl.pallas_call(
        functools.partial(_dispatch_kernel, tm=tm, top_k=top_k),
        grid_spec=pltpu.PrefetchScalarGridSpec(
            num_scalar_prefetch=1,
            grid=(t // tm,),
            in_specs=[pl.BlockSpec((tm, dw), lambda i, dest: (i, 0))],
            out_specs=pl.BlockSpec(memory_space=pl.ANY),
            scratch_shapes=[pltpu.SemaphoreType.DMA(())],
        ),
        out_shape=jax.ShapeDtypeStruct((rows_out, dw), U32),
        compiler_params=pltpu.CompilerParams(dimension_semantics=("arbitrary",),
                                             vmem_limit_bytes=VMEM_LIMIT, has_side_effects=True),
        name="dispatch",
    )(dest, h_packed)


def _ffn_kernel(blk_e_ref, blk_valid_ref, nused_ref, x_ref, wg_ref, wu_ref, wd_ref, y_ref,
                wg_bf, wu_bf, wd_bf):
    j = pl.program_id(0)

    @pl.when(j < nused_ref[0])
    def _():
        @pl.when(jnp.logical_or(j == 0, blk_e_ref[j] != blk_e_ref[jnp.maximum(j - 1, 0)]))
        def _():
            wg_bf[...] = wg_ref[0, 0].astype(BF16)
            wu_bf[...] = wu_ref[0, 0].astype(BF16)
            wd_bf[...] = wd_ref[0, 0].astype(BF16)

        rows = lax.broadcasted_iota(I32, x_ref.shape, 0)
        packed = jnp.where(rows < blk_valid_ref[j], x_ref[...], jnp.uint32(0))
        a, b = _unpack_bf16_pair(packed)
        x = jnp.concatenate([a.astype(BF16), b.astype(BF16)], axis=1)
        gate = jnp.dot(x, wg_bf[...], preferred_element_type=F32)
        up = jnp.dot(x, wu_bf[...], preferred_element_type=F32)
        hid = (gate * _sigmoid(gate) * up).astype(BF16)
        y = jnp.dot(hid, wd_bf[...], preferred_element_type=F32)
        half = y.shape[1] // 2
        y_ref[...] = _pack_bf16_pair(y[:, :half], y[:, half:])


def expert_ffn(blk_e, blk_valid, nused, xs, w_gate, w_up, w_down, *, layer, tb):
    p, dw = xs.shape
    _, _, d, ff = w_gate.shape

    def row_map(j, be, bv, nu):
        return (jnp.minimum(j, nu[0] - 1), 0)

    def w_map(j, be, bv, nu):
        return (layer, be[j], 0, 0)

    return pl.pallas_call(
        _ffn_kernel,
        grid_spec=pltpu.PrefetchScalarGridSpec(
            num_scalar_prefetch=3,
            grid=(p // tb,),
            in_specs=[
                pl.BlockSpec((tb, dw), row_map),
                pl.BlockSpec((1, 1, d, ff), w_map),
                pl.BlockSpec((1, 1, d, ff), w_map),
                pl.BlockSpec((1, 1, ff, d), w_map),
            ],
            out_specs=pl.BlockSpec((tb, dw), row_map),
            scratch_shapes=[pltpu.VMEM((d, ff), BF16), pltpu.VMEM((d, ff), BF16), pltpu.VMEM((ff, d), BF16)],
        ),
        out_shape=jax.ShapeDtypeStruct((p, dw), U32),
        compiler_params=_params("arbitrary"),
        name="ffn",
    )(blk_e, blk_valid, nused, xs, w_gate, w_up, w_down)


def _combine_kernel(dest_ref, x_ref, route_ref, g_ref, ys_ref, o_ref, buf_ref, sem, *, tm, top_k, final_norm):
    i = pl.program_id(0)
    slot = i % 2

    def gather_tile(tile, slot):
        base = tile * tm * top_k

        def issue(t, carry):
            for k in range(top_k):
                pltpu.make_async_copy(ys_ref.at[pl.ds(dest_ref[base + t * top_k + k], 1)],
                                      buf_ref.at[slot, k, pl.ds(t, 1)], sem.at[slot]).start()
            return carry

        lax.fori_loop(0, tm, issue, 0, unroll=DMA_ISSUE_UNROLL)

    @pl.when(i == 0)
    def _():
        gather_tile(i, slot)

    @pl.when(i + 1 < pl.num_programs(0))
    def _():
        gather_tile(i + 1, 1 - slot)

    for k in range(top_k):
        pltpu.make_async_copy(ys_ref.at[pl.ds(0, tm)], buf_ref.at[slot, k], sem.at[slot]).wait()

    route = route_ref[...]
    out = x_ref[...]
    for k in range(top_k):
        a, b = _unpack_bf16_pair(buf_ref[slot, k])
        out = out + route[:, 2 + k:3 + k] * jnp.concatenate([a, b], axis=1)
    if final_norm:
        ms = jnp.mean(out * out, axis=-1, keepdims=True)
        out = out * lax.rsqrt(ms + EPS) * g_ref[...]
    o_ref[...] = out


def combine(dest, x, route, g, ys, *, tm, top_k, final_norm):
    t, d = x.shape
    dw = ys.shape[1]
    return pl.pallas_call(
        functools.partial(_combine_kernel, tm=tm, top_k=top_k, final_norm=final_norm),
        grid_spec=pltpu.PrefetchScalarGridSpec(
            num_scalar_prefetch=1,
            grid=(t // tm,),
            in_specs=[
                pl.BlockSpec((tm, d), lambda i, dest: (i, 0)),
                pl.BlockSpec((tm, LANES), lambda i, dest: (i, 0)),
                pl.BlockSpec((1, d), lambda i, dest: (0, 0)),
                pl.BlockSpec(memory_space=pl.ANY),
            ],
            out_specs=pl.BlockSpec((tm, d), lambda i, dest: (i, 0)),
            scratch_shapes=[pltpu.VMEM((2, top_k, tm, dw), U32), pltpu.SemaphoreType.DMA((2,))],
        ),
        out_shape=jax.ShapeDtypeStruct((t, d), F32),
        compiler_params=_params("arbitrary"),
        name="combine",
    )(dest, x, route, g.reshape(1, d), ys)


def moe_layer(x_new, h_packed, route, counts, w_gate, w_up, w_down, g_final, *, layer, tb, tm_dispatch, tm_combine,
              final_norm):
    t = x_new.shape[0]
    top_k = 2
    cnt = counts[0, :MOE_EXPERTS].astype(I32)
    padded = ((cnt + tb - 1) // tb) * tb
    pends = jnp.cumsum(padded)
    pstarts = pends - padded
    expert_ids = jnp.arange(MOE_EXPERTS, dtype=I32)
    experts = route[:, 0:top_k].astype(I32)
    ranks = route[:, 4:4 + top_k].astype(I32)
    start_of = jnp.sum(jnp.where(experts[..., None] == expert_ids, pstarts, 0), axis=-1)
    dest = (start_of + ranks).reshape(t * top_k)
    rows_out = t * top_k + MOE_EXPERTS * tb
    nb = rows_out // tb
    blk_start = jnp.arange(nb, dtype=I32) * tb
    blk_e = jnp.minimum(jnp.sum((blk_start[:, None] >= pends[None, :]).astype(I32), axis=1), MOE_EXPERTS - 1)
    blk_onehot = blk_e[:, None] == expert_ids
    blk_end = jnp.sum(jnp.where(blk_onehot, pstarts + cnt, 0), axis=1)
    blk_valid = jnp.clip(blk_end - blk_start, 0, tb).astype(I32)
    nused = (pends[-1:] // tb).astype(I32)

    xs = dispatch(dest, h_packed, rows_out=rows_out, tm=tm_dispatch, top_k=top_k)
    ys = expert_ffn(blk_e, blk_valid, nused, xs, w_gate, w_up, w_down, layer=layer, tb=tb)
    return combine(dest, x_new, route, g_final, ys, tm=tm_combine, top_k=top_k, final_norm=final_norm)


def _rope_kernel(pos_ref, inv_ref, cos_ref, sin_ref):
    ang = pos_ref[...].astype(F32) * inv_ref[...]
    cos_ref[...] = jnp.cos(ang)
    sin_ref[...] = jnp.sin(ang)


def rope_table(positions, half, *, tm):
    t = positions.size
    inv_freq = (ROPE_BASE ** (-jnp.arange(half, dtype=F32) / half)).reshape(1, half)
    return pl.pallas_call(
        _rope_kernel,
        grid=(t // tm,),
        in_specs=[pl.BlockSpec((tm, 1), lambda i: (i, 0)), pl.BlockSpec((1, half), lambda i: (0, 0))],
        out_specs=[pl.BlockSpec((tm, half), lambda i: (i, 0))] * 2,
        out_shape=[jax.ShapeDtypeStruct((t, half), F32)] * 2,
        compiler_params=_params("arbitrary"),
        name="rope_table",
    )(positions.reshape(t, 1), inv_freq)


def _retention_kernel(lg_ref, q_ref, k_ref, v_ref, g_ref, cos_ref, sin_ref, gain_ref, o_ref,
                      state_ref, rhs_ref, kscale_ref, eps_ref, *, c, hp):
    dk = q_ref.shape[1] // hp
    dv = v_ref.shape[1] // hp
    half = dk // 2
    heads = range(hp)
    lgs = [lg_ref[pl.program_id(1) * hp + h] for h in heads]
    pos = lax.broadcasted_iota(I32, (c, dk), 0).astype(F32)

    @pl.when(pl.program_id(2) == 0)
    def _():
        state_ref[...] = jnp.zeros_like(state_ref)
        for h in heads:
            rhs_ref[h, c:, :] = jnp.zeros((dk, dv), BF16)
            kscale_ref[h] = jnp.exp(-(pos + 1.0) * lgs[h]) * (dk ** -0.5)
            eps_ref[h] = jnp.exp(-2.0 * (pos[:, 0:1] + 1.0) * lgs[h]) * (dv * EPS)

    cos = cos_ref[...]
    sin = sin_ref[...]

    def rotate(x):
        x1, x2 = x[:, :half], x[:, half:]
        return jnp.concatenate([x1 * cos - x2 * sin, x2 * cos + x1 * sin], axis=1)

    causal = lax.broadcasted_iota(I32, (c, c), 0) >= lax.broadcasted_iota(I32, (c, c), 1)

    def scores(h):
        qr = rotate(q_ref[:, h * dk:(h + 1) * dk].astype(F32)).astype(BF16)
        kp = (rotate(k_ref[:, h * dk:(h + 1) * dk].astype(F32)) * kscale_ref[h]).astype(BF16)
        s = lax.dot_general(qr, kp, (((1,), (1,)), ((), ())), preferred_element_type=F32)
        return qr, kp, s

    ahead = scores(0)
    for h in heads:
        qr, kp, s = ahead
        if h + 1 < hp:
            ahead = scores(h + 1)
        v = v_ref[:, h * dv:(h + 1) * dv]
        rhs_ref[h, :c, :] = v
        kv = lax.dot_general(kp, v, (((0,), (0,)), ((), ())), preferred_element_type=F32)
        s = jnp.where(causal, s, 0.0).astype(BF16)
        y = jnp.dot(jnp.concatenate([s, qr], axis=1), rhs_ref[h], preferred_element_type=F32)

        state = (state_ref[h] + kv) * jnp.exp(jnp.zeros((1, dv), F32) + c * lgs[h])
        state_ref[h] = state
        rhs_ref[h, c:, :] = state.astype(BF16)

        yc = y - jnp.sum(y, axis=-1, keepdims=True) * (1.0 / dv)
        ss = jnp.sum(yc * yc, axis=-1, keepdims=True)
        yn = yc * lax.rsqrt(ss + eps_ref[h]) * (gain_ref[0, h:h + 1, :] * (dv ** 0.5))
        g = g_ref[:, h * dv:(h + 1) * dv]
        gate = g / (1.0 + jnp.exp(-g))
        o_ref[:, h * dv:(h + 1) * dv] = yn.astype(o_ref.dtype) * gate


def retention(proj, cos, sin, gain, *, batch, seq, heads, dk, dv, c, hp):
    nchunk = seq // c
    log_g = jnp.log1p(-jnp.power(2.0, -5.0 - jnp.arange(heads, dtype=F32)))
    groups = heads // hp
    k_col = groups
    v_col = 2 * heads * dk // (hp * dv)
    g_col = v_col + groups
    half = dk // 2

    def rows(b, n):
        return b * nchunk + n

    return pl.pallas_call(
        functools.partial(_retention_kernel, c=c, hp=hp),
        grid_spec=pltpu.PrefetchScalarGridSpec(
            num_scalar_prefetch=1,
            grid=(batch, groups, nchunk),
            in_specs=[
                pl.BlockSpec((c, hp * dk), lambda b, h, n, lg: (rows(b, n), h)),
                pl.BlockSpec((c, hp * dk), lambda b, h, n, lg: (rows(b, n), k_col + h)),
                pl.BlockSpec((c, hp * dv), lambda b, h, n, lg: (rows(b, n), v_col + h)),
                pl.BlockSpec((c, hp * dv), lambda b, h, n, lg: (rows(b, n), g_col + h)),
                pl.BlockSpec((c, half), lambda b, h, n, lg: (rows(b, n), 0)),
                pl.BlockSpec((c, half), lambda b, h, n, lg: (rows(b, n), 0)),
                pl.BlockSpec((1, hp, dv), lambda b, h, n, lg: (h, 0, 0)),
            ],
            out_specs=pl.BlockSpec((c, hp * dv), lambda b, h, n, lg: (rows(b, n), h)),
            scratch_shapes=[
                pltpu.VMEM((hp, dk, dv), F32),
                pltpu.VMEM((hp, c + dk, dv), BF16),
                pltpu.VMEM((hp, c, dk), F32),
                pltpu.VMEM((hp, c, 1), F32),
            ],
        ),
        out_shape=jax.ShapeDtypeStruct((batch * seq, heads * dv), BF16),
        compiler_params=_params("arbitrary", "arbitrary", "arbitrary"),
        name="retention",
    )(log_g, proj, proj, proj, proj, cos, sin, gain.reshape(groups, hp, dv))


def _router_params(wg, bg, we, be):
    d = wg.shape[0]
    n = MOE_GROUPS + MOE_EXPERTS
    w = jnp.concatenate([wg, we, jnp.zeros((d, LANES - n), F32)], axis=1)
    b = jnp.concatenate([bg, be, jnp.zeros((LANES - n,), F32)]).reshape(1, LANES)
    w_hi = w.astype(BF16)
    w_lo = (w - w_hi.astype(F32)).astype(BF16)
    return jnp.concatenate([w_hi, w_lo], axis=1), b


def _pick(total, want):
    t = min(want, total)
    while total % t:
        t //= 2
    return t


def kernel(x, positions, ev_norm, ev_w_in, ev_conv_w, ev_conv_b, ev_gate_a_w, ev_gate_a_b, ev_gate_x_w, ev_gate_x_b, ev_lru_lambda, ev_w_out, od_norm, od_w_in, od_gn_gain, od_w_out, ffn_norm, router_group_w, router_group_b, router_expert_w, router_expert_b, expert_w_gate, expert_w_up, expert_w_down, final_norm):
    batch, seq, d = x.shape
    t = batch * seq
    depth = ffn_norm.shape[0]
    lru_w = ev_conv_w.shape[2]
    sb_w = (ev_w_in.shape[2] - 2 * lru_w) // 3
    head_dim = sb_w // SB_HEADS
    dv = od_gn_gain.shape[2]
    dk = (od_w_in.shape[2] - 2 * RET_HEADS * dv) // (2 * RET_HEADS)

    tm_proj = _pick(t, 1024)
    tm_out = _pick(t, 512)
    tm_dispatch = _pick(t, 2048)
    tm_combine = _pick(t, 512)
    tb = 512

    xf = x.reshape(t, d)
    for layer in range(depth):
        i = layer // 2
        if layer % 2 == 0:
            proj = norm_matmul(xf, ev_norm[i], ev_w_in[i].astype(BF16), tm=_pick(t, 512), tn=ev_w_in.shape[2])
            gate_w = jnp.concatenate([ev_gate_a_w[i], ev_gate_x_w[i]], axis=2).astype(BF16)
            gate_b = jnp.stack([ev_gate_a_b[i].reshape(lru_w), ev_gate_x_b[i].reshape(lru_w)])
            y_a = lru_branch(proj, ev_conv_w[i], ev_conv_b[i], gate_w, gate_b, ev_lru_lambda[i],
                             batch=batch, seq=seq, width=lru_w, tc=_pick(seq, 1024))
            col0 = 2 * lru_w // head_dim
            y_b = stick_breaking(proj, batch=batch, seq=seq, heads=SB_HEADS, head_dim=head_dim,
                                 q_col=col0, k_col=col0 + SB_HEADS, v_col=col0 + 2 * SB_HEADS,
                                 qb=SBA_QUERY_BLOCK, nchain=_pick(seq // SBA_QUERY_BLOCK, SBA_CHAINS),
                                 wk=SBA_FIRST_KEYS, fk=SBA_MORE_KEYS)
            w_out = ev_w_out[i].astype(BF16)
            ys, ws = [y_a, y_b], [w_out[:lru_w], w_out[lru_w:]]
        else:
            proj = norm_matmul(xf, od_norm[i], od_w_in[i].astype(BF16), tm=tm_proj, tn=_pick(od_w_in.shape[2], 2048))
            cos, sin = rope_table(positions, dk // 2, tm=_pick(t, 1024))
            y = retention(proj, cos, sin, od_gn_gain[i], batch=batch, seq=seq, heads=RET_HEADS,
                          dk=dk, dv=dv, c=_pick(seq, 256), hp=RET_HEADS_PER_STEP)
            ys, ws = [y], [od_w_out[i].astype(BF16)]

        rw, rb = _router_params(router_group_w[layer], router_group_b[layer],
                                router_expert_w[layer], router_expert_b[layer])
        x_new, h_packed, route, counts = out_router(ys, ws, xf, ffn_norm[layer], rw, rb, tm=tm_out)
        last = layer == depth - 1
        xf = moe_layer(x_new, h_packed, route, counts, expert_w_gate, expert_w_up, expert_w_down, final_norm,
                       layer=layer, tb=tb, tm_dispatch=tm_dispatch, tm_combine=tm_combine, final_norm=last)
    return xf.reshape(batch, seq, d)
```

```python
import functools

import jax
import jax.numpy as jnp
from jax import lax
from jax.experimental import pallas as pl
from jax.experimental.pallas import tpu as pltpu

F32 = jnp.float32
BF16 = jnp.bfloat16
U32 = jnp.uint32
I32 = jnp.int32

LANES = 128
SUBLANES = 8
VMEM_BYTES_V7X = 64 * 1024 * 1024
VMEM_LIMIT = VMEM_BYTES_V7X * 7 // 8

EPS = 1e-6
LRU_BLOCKS = 8
CONV_WIDTH = 4
LRU_C = 8.0
SB_HEADS = 8
RET_HEADS = 8
ROPE_BASE = 10000.0
MOE_GROUPS = 4
MOE_EXPERTS_PER_GROUP = 8
MOE_EXPERTS = MOE_GROUPS * MOE_EXPERTS_PER_GROUP

SBA_QUERY_BLOCK = 64
SBA_CHAINS = 32
SBA_FIRST_KEYS = 256
SBA_MORE_KEYS = 128
LOG2E = 1.4426950408889634

RET_HEADS_PER_STEP = 8

DMA_ISSUE_UNROLL = 8

F32_EXP_ZERO = -104.0

HI16 = 0xFFFF0000


def _params(*sem):
    return pltpu.CompilerParams(dimension_semantics=sem, vmem_limit_bytes=VMEM_LIMIT)


def _resident(shape):
    return pl.BlockSpec(shape, lambda *_: (0,) * len(shape), pipeline_mode=pl.Buffered(1))


def _pack_bf16_pair(a, b):
    ua = lax.bitcast_convert_type(a.astype(BF16).astype(F32), U32)
    ub = lax.bitcast_convert_type(b.astype(BF16).astype(F32), U32)
    return (ua >> 16) | (ub & jnp.uint32(HI16))


def _unpack_bf16_pair(w):
    a = lax.bitcast_convert_type(w << 16, F32)
    b = lax.bitcast_convert_type(w & jnp.uint32(HI16), F32)
    return a, b


def _norm_matmul_kernel(x_ref, g_ref, w_ref, o_ref, h_ref):
    @pl.when(pl.program_id(1) == 0)
    def _():
        x = x_ref[...]
        ms = jnp.mean(x * x, axis=-1, keepdims=True)
        h_ref[...] = (x * lax.rsqrt(ms + EPS) * g_ref[...]).astype(h_ref.dtype)

    o_ref[...] = jnp.dot(h_ref[...], w_ref[...], preferred_element_type=F32).astype(o_ref.dtype)


def norm_matmul(x, g, w, *, tm, tn):
    t, d = x.shape
    n = w.shape[1]
    w_spec = _resident((d, n)) if tn == n else pl.BlockSpec((d, tn), lambda i, j: (0, j))
    return pl.pallas_call(
        _norm_matmul_kernel,
        grid=(t // tm, n // tn),
        in_specs=[
            pl.BlockSpec((tm, d), lambda i, j: (i, 0)),
            pl.BlockSpec((1, d), lambda i, j: (0, 0)),
            w_spec,
        ],
        out_specs=pl.BlockSpec((tm, tn), lambda i, j: (i, j)),
        out_shape=jax.ShapeDtypeStruct((t, n), BF16),
        scratch_shapes=[pltpu.VMEM((tm, d), BF16)],
        compiler_params=_params("arbitrary", "arbitrary"),
        name="norm_matmul",
    )(x, g.reshape(1, d), w)


def _sigmoid(x):
    return 1.0 / (1.0 + jnp.exp(-x))


def _lru_kernel(xr_ref, gr_ref, cw_ref, cb_ref, gw_ref, gb_ref, lam_ref, o_ref, tail_ref, h_ref, *, tc):
    @pl.when(pl.program_id(1) == 0)
    def _():
        tail_ref[...] = jnp.zeros_like(tail_ref)
        h_ref[...] = jnp.zeros_like(h_ref)

    c = xr_ref.shape[1]
    bw = c // LRU_BLOCKS
    x = xr_ref[...].astype(F32)
    xe = jnp.concatenate([tail_ref[...], x], axis=0)
    cw = cw_ref[...]
    xc = cb_ref[...] + cw[3:4] * x
    for k in range(CONV_WIDTH - 1):
        off = SUBLANES - (CONV_WIDTH - 1 - k)
        xc = xc + cw[k:k + 1] * xe[off:off + tc]
    tail_ref[...] = x[tc - SUBLANES:, :]

    xcb = xc.astype(BF16)
    parts = [jnp.dot(xcb[:, g * bw:(g + 1) * bw], gw_ref[g], preferred_element_type=F32)
             for g in range(LRU_BLOCKS)]
    gb = gb_ref[...]
    r = _sigmoid(jnp.concatenate([p[:, :bw] for p in parts], axis=1) + gb[0:1])
    ig = _sigmoid(jnp.concatenate([p[:, bw:] for p in parts], axis=1) + gb[1:2])

    nlam = -lam_ref[...]
    softplus_nlam = jnp.maximum(nlam, 0.0) + jnp.log1p(jnp.exp(-jnp.abs(nlam)))
    log_a = (-LRU_C) * r * softplus_nlam
    a = jnp.exp(log_a)
    u = jnp.sqrt(1.0 - a * a) * (ig * xc)

    groups = tc // SUBLANES
    a = a.reshape(groups, SUBLANES, c)
    u = u.reshape(groups, SUBLANES, c)
    row_in_group = lax.broadcasted_iota(I32, (groups, SUBLANES, c), 1)
    d = 1
    while d < SUBLANES:
        keep = row_in_group >= d
        a_prev = jnp.where(keep, pltpu.roll(a, d, 1), 1.0)
        u_prev = jnp.where(keep, pltpu.roll(u, d, 1), 0.0)
        u = a * u_prev + u
        a = a * a_prev
        d *= 2
    carry = h_ref[...]
    hs = []
    for g in range(groups):
        hg = u[g] + a[g] * carry
        carry = hg[SUBLANES - 1:SUBLANES, :]
        hs.append(hg)
    h = jnp.concatenate(hs, axis=0)
    h_ref[...] = carry
    o_ref[...] = (h * jax.nn.gelu(gr_ref[...].astype(F32))).astype(o_ref.dtype)


def lru_branch(proj, conv_w, conv_b, gate_w, gate_b, lam, *, batch, seq, width, tc):
    nchunk = seq // tc
    return pl.pallas_call(
        functools.partial(_lru_kernel, tc=tc),
        grid=(batch, nchunk),
        in_specs=[
            pl.BlockSpec((tc, width), lambda b, i: (b * nchunk + i, 0)),
            pl.BlockSpec((tc, width), lambda b, i: (b * nchunk + i, 1)),
            pl.BlockSpec((CONV_WIDTH, width), lambda b, i: (0, 0)),
            pl.BlockSpec((1, width), lambda b, i: (0, 0)),
            pl.BlockSpec(gate_w.shape, lambda b, i: (0, 0, 0)),
            pl.BlockSpec((2, width), lambda b, i: (0, 0)),
            pl.BlockSpec((1, width), lambda b, i: (0, 0)),
        ],
        out_specs=pl.BlockSpec((tc, width), lambda b, i: (b * nchunk + i, 0)),
        out_shape=jax.ShapeDtypeStruct((batch * seq, width), BF16),
        scratch_shapes=[pltpu.VMEM((SUBLANES, width), F32), pltpu.VMEM((1, width), F32)],
        compiler_params=_params("arbitrary", "arbitrary"),
        name="lru",
    )(proj, proj, conv_w, conv_b.reshape(1, width), gate_w, gate_b, lam.reshape(1, width))


def _sba_blocks(qs, kbs, vbs, later, masks, runs, *, scale):
    n = len(qs)
    qb = qs[0].shape[0]
    zs = [lax.dot_general(q, kb, (((1,), (1,)), ((), ())), preferred_element_type=F32) * (scale * LOG2E)
          for q, kb in zip(qs, kbs)]
    softplus = [jnp.maximum(z, 0.0) + jnp.log(1.0 + jnp.exp2(-jnp.abs(z))) * LOG2E for z in zs]
    log_1m = [jnp.where(m, -s, 0.0) for m, s in zip(masks, softplus)]
    his = [x.astype(BF16) for x in log_1m]
    los = [(x - h.astype(F32)).astype(BF16) for x, h in zip(log_1m, his)]
    sums = jnp.dot(jnp.concatenate(his + los, axis=0), later, preferred_element_type=F32)
    outs = []
    for c in range(n):
        rest = sums[c * qb:(c + 1) * qb] + sums[(n + c) * qb:(n + c + 1) * qb]
        log_w = (zs[c] - softplus[c]) + rest
        if runs[c] is not None:
            log_w = log_w + runs[c]
        w = jnp.where(masks[c], jnp.exp2(log_w), 0.0)
        outs.append((w.astype(BF16), rest[:, 0:1] + log_1m[c][:, 0:1]))
    return [(jnp.dot(w, vb, preferred_element_type=F32), total) for (w, total), vb in zip(outs, vbs)]


def _sba_kernel(q_ref, k_ref, v_ref, o_ref, later_ref, acc_ref, run_ref, *, qb, nchain, wk, fk, scale):
    step_id = pl.program_id(2)
    dead_below = F32_EXP_ZERO * LOG2E

    @pl.when(step_id == 0)
    def _():
        r = lax.broadcasted_iota(I32, (wk, wk), 0)
        c = lax.broadcasted_iota(I32, (wk, wk), 1)
        later_ref[...] = jnp.where(r > c, 1.0, 0.0).astype(BF16)

    col_minus_row = (lax.broadcasted_iota(I32, (qb, wk), 1) - lax.broadcasted_iota(I32, (qb, wk), 0))

    def first_key(c):
        q0 = (step_id * nchain + c) * qb
        return q0, pl.multiple_of(jnp.maximum(q0 + qb - wk, 0), qb)

    lefts = []
    masks = []
    for c in range(nchain):
        q0, left = first_key(c)
        lefts.append(left)
        masks.append(col_minus_row < q0 - left)
    first = _sba_blocks([q_ref[c * qb:(c + 1) * qb, :] for c in range(nchain)],
                        [k_ref[pl.ds(left, wk), :] for left in lefts],
                        [v_ref[pl.ds(left, wk), :] for left in lefts],
                        later_ref[...], masks, [None] * nchain, scale=scale)
    worst = first[0][1]
    for c, (out, run) in enumerate(first):
        o_ref[c * qb:(c + 1) * qb, :] = out.astype(o_ref.dtype)
        worst = jnp.maximum(worst, run)

    @pl.when(jnp.max(worst) > dead_below)
    def _():
        for c, (out, run) in enumerate(first):
            acc_ref[c] = out
            run_ref[c] = run
        key_col = lax.broadcasted_iota(I32, (qb, fk), 1)

        def walk_left(c, carry):
            rows = pl.ds(pl.multiple_of(c * qb, qb), qb)

            def cond(state):
                left, live = state
                return jnp.logical_and(left > 0, live)

            def body(state):
                left, _ = state
                start = pl.multiple_of(jnp.maximum(left - fk, 0), qb)
                run = run_ref[c]
                (out, more), = _sba_blocks([q_ref[rows, :]], [k_ref[pl.ds(start, fk), :]],
                                           [v_ref[pl.ds(start, fk), :]], later_ref[0:fk, 0:fk],
                                           [key_col < left - start], [run], scale=scale)
                acc_ref[c] += out
                run_ref[c] = run + more
                return start, jnp.max(run + more) > dead_below

            lax.while_loop(cond, body, (first_key(c)[1], jnp.max(run_ref[c]) > dead_below))
            o_ref[rows, :] = acc_ref[c].astype(o_ref.dtype)
            return carry

        lax.fori_loop(0, nchain, walk_left, 0)


def stick_breaking(proj, *, batch, seq, heads, head_dim, q_col, k_col, v_col, qb, nchain, wk, fk):
    nstep = seq // (qb * nchain)
    rows = qb * nchain
    return pl.pallas_call(
        functools.partial(_sba_kernel, qb=qb, nchain=nchain, wk=wk, fk=fk, scale=head_dim ** -0.5),
        grid=(batch, heads, nstep),
        in_specs=[
            pl.BlockSpec((rows, head_dim), lambda b, h, i: (b * nstep + i, q_col + h)),
            pl.BlockSpec((seq, head_dim), lambda b, h, i: (b, k_col + h)),
            pl.BlockSpec((seq, head_dim), lambda b, h, i: (b, v_col + h)),
        ],
        out_specs=pl.BlockSpec((rows, head_dim), lambda b, h, i: (b * nstep + i, h)),
        out_shape=jax.ShapeDtypeStruct((batch * seq, heads * head_dim), BF16),
        scratch_shapes=[
            pltpu.VMEM((wk, wk), BF16),
            pltpu.VMEM((nchain, qb, head_dim), F32),
            pltpu.VMEM((nchain, qb, 1), F32),
        ],
        compiler_params=_params("arbitrary", "arbitrary", "arbitrary"),
        name="sba",
    )(proj, proj, proj)


def _out_router_kernel(*refs, n_in):
    y_refs = refs[:n_in]
    w_refs = refs[n_in:2 * n_in]
    x_ref, g_ref, rw_ref, rb_ref = refs[2 * n_in:2 * n_in + 4]
    xo_ref, hp_ref, route_ref, cnt_ref, carry_ref = refs[2 * n_in + 4:]

    @pl.when(pl.program_id(0) == 0)
    def _():
        carry_ref[...] = jnp.zeros_like(carry_ref)

    acc = x_ref[...]
    for y_ref, w_ref in zip(y_refs, w_refs):
        acc = acc + jnp.dot(y_ref[...], w_ref[...], preferred_element_type=F32)
    xo_ref[...] = acc

    tm, d = acc.shape
    ms = jnp.mean(acc * acc, axis=-1, keepdims=True)
    h = acc * lax.rsqrt(ms + EPS) * g_ref[...]
    hp_ref[...] = _pack_bf16_pair(h[:, :d // 2], h[:, d // 2:])

    h_hi = h.astype(BF16)
    h_lo = (h - h_hi.astype(F32)).astype(BF16)
    logits = (jnp.dot(h_hi, rw_ref[:, :LANES], preferred_element_type=F32)
              + jnp.dot(h_lo, rw_ref[:, :LANES], preferred_element_type=F32)
              + jnp.dot(h_hi, rw_ref[:, LANES:], preferred_element_type=F32)
              + rb_ref[...])

    lane = lax.broadcasted_iota(I32, (tm, LANES), 1)
    neg = jnp.float32(-jnp.inf)
    big = jnp.int32(LANES)

    def first_argmax(vals, mask):
        m = jnp.max(jnp.where(mask, vals, neg), axis=-1, keepdims=True)
        idx = jnp.min(jnp.where(jnp.logical_and(mask, vals == m), lane, big), axis=-1, keepdims=True)
        return m, idx

    is_group = lane < MOE_GROUPS
    g_max, g_idx = first_argmax(logits, is_group)
    g_w = 1.0 / jnp.sum(jnp.where(is_group, jnp.exp(logits - g_max), 0.0), axis=-1, keepdims=True)
    e_lo = MOE_GROUPS + g_idx * MOE_EXPERTS_PER_GROUP
    in_group = jnp.logical_and(lane >= e_lo, lane < e_lo + MOE_EXPERTS_PER_GROUP)
    l1, i1 = first_argmax(logits, in_group)
    l2, i2 = first_argmax(logits, jnp.logical_and(in_group, lane != i1))
    t = jnp.exp(l2 - l1)
    w1 = g_w / (1.0 + t)
    w2 = g_w * t / (1.0 + t)
    e1 = i1 - MOE_GROUPS
    e2 = i2 - MOE_GROUPS

    onehot = jnp.where(jnp.logical_or(lane == e1, lane == e2), 1.0, 0.0)
    lower = jnp.where(lax.broadcasted_iota(I32, (tm, tm), 1) < lax.broadcasted_iota(I32, (tm, tm), 0),
                      1.0, 0.0).astype(BF16)
    before = jnp.dot(lower, onehot.astype(BF16), preferred_element_type=F32) + carry_ref[...]
    rank1 = jnp.sum(jnp.where(lane == e1, before, 0.0), axis=-1, keepdims=True)
    rank2 = jnp.sum(jnp.where(lane == e2, before, 0.0), axis=-1, keepdims=True)
    new_carry = before[tm - 1:tm, :] + onehot[tm - 1:tm, :]
    carry_ref[...] = new_carry
    cnt_ref[...] = jnp.broadcast_to(new_carry, cnt_ref.shape)

    route = jnp.where(lane == 0, e1.astype(F32), 0.0)
    route = jnp.where(lane == 1, e2.astype(F32), route)
    route = jnp.where(lane == 2, w1, route)
    route = jnp.where(lane == 3, w2, route)
    route = jnp.where(lane == 4, rank1, route)
    route = jnp.where(lane == 5, rank2, route)
    route_ref[...] = route


def out_router(ys, ws, x, g, rw, rb, *, tm):
    t, d = x.shape
    n_in = len(ys)
    in_specs = ([pl.BlockSpec((tm, y.shape[1]), lambda i: (i, 0)) for y in ys]
                + [_resident(w.shape) for w in ws]
                + [pl.BlockSpec((tm, d), lambda i: (i, 0)),
                   _resident((1, d)), _resident(rw.shape), _resident((1, LANES))])
    return pl.pallas_call(
        functools.partial(_out_router_kernel, n_in=n_in),
        grid=(t // tm,),
        in_specs=in_specs,
        out_specs=[
            pl.BlockSpec((tm, d), lambda i: (i, 0)),
            pl.BlockSpec((tm, d // 2), lambda i: (i, 0)),
            pl.BlockSpec((tm, LANES), lambda i: (i, 0)),
            pl.BlockSpec((SUBLANES, LANES), lambda i: (0, 0)),
        ],
        out_shape=[
            jax.ShapeDtypeStruct((t, d), F32),
            jax.ShapeDtypeStruct((t, d // 2), U32),
            jax.ShapeDtypeStruct((t, LANES), F32),
            jax.ShapeDtypeStruct((SUBLANES, LANES), F32),
        ],
        scratch_shapes=[pltpu.VMEM((1, LANES), F32)],
        compiler_params=_params("arbitrary"),
        name="out_router",
    )(*ys, *ws, x, g.reshape(1, d), rw, rb)


def _dispatch_kernel(dest_ref, h_ref, xs_ref, sem, *, tm, top_k):
    base = pl.program_id(0) * tm * top_k

    def row_copy(t, k):
        return pltpu.make_async_copy(h_ref.at[pl.ds(t, 1)],
                                     xs_ref.at[pl.ds(dest_ref[base + t * top_k + k], 1)], sem)

    def issue(t, carry):
        for k in range(top_k):
            row_copy(t, k).start()
        return carry

    lax.fori_loop(0, tm, issue, 0, unroll=DMA_ISSUE_UNROLL)

    for k in range(top_k):
        pltpu.make_async_copy(h_ref, xs_ref.at[pl.ds(0, tm)], sem).wait()


def dispatch(dest, h_packed, *, rows_out, tm, top_k):
    t, dw = h_packed.shape
    return pl.pallas_call(
        functools.partial(_dispatch_kernel, tm=tm, top_k=top_k),
        grid_spec=pltpu.PrefetchScalarGridSpec(
            num_scalar_prefetch=1,
            grid=(t // tm,),
            in_specs=[pl.BlockSpec((tm, dw), lambda i, dest: (i, 0))],
            out_specs=pl.BlockSpec(memory_space=pl.ANY),
            scratch_shapes=[pltpu.SemaphoreType.DMA(())],
        ),
        out_shape=jax.ShapeDtypeStruct((rows_out, dw), U32),
        compiler_params=pltpu.CompilerParams(dimension_semantics=("arbitrary",),
                                             vmem_limit_bytes=VMEM_LIMIT, has_side_effects=True),
        name="dispatch",
    )(dest, h_packed)


def _ffn_kernel(blk_e_ref, blk_valid_ref, nused_ref, x_ref, wg_ref, wu_ref, wd_ref, y_ref,
                wg_bf, wu_bf, wd_bf):
    j = pl.program_id(0)

    @pl.when(j < nused_ref[0])
    def _():
        @pl.when(jnp.logical_or(j == 0, blk_e_ref[j] != blk_e_ref[jnp.maximum(j - 1, 0)]))
        def _():
            wg_bf[...] = wg_ref[0, 0].astype(BF16)
            wu_bf[...] = wu_ref[0, 0].astype(BF16)
            wd_bf[...] = wd_ref[0, 0].astype(BF16)

        rows = lax.broadcasted_iota(I32, x_ref.shape, 0)
        packed = jnp.where(rows < blk_valid_ref[j], x_ref[...], jnp.uint32(0))
        a, b = _unpack_bf16_pair(packed)
        x = jnp.concatenate([a.astype(BF16), b.astype(BF16)], axis=1)
        gate = jnp.dot(x, wg_bf[...], preferred_element_type=F32)
        up = jnp.dot(x, wu_bf[...], preferred_element_type=F32)
        hid = (gate * _sigmoid(gate) * up).astype(BF16)
        y = jnp.dot(hid, wd_bf[...], preferred_element_type=F32)
        half = y.shape[1] // 2
        y_ref[...] = _pack_bf16_pair(y[:, :half], y[:, half:])


def expert_ffn(blk_e, blk_valid, nused, xs, w_gate, w_up, w_down, *, layer, tb):
    p, dw = xs.shape
    _, _, d, ff = w_gate.shape

    def row_map(j, be, bv, nu):
        return (jnp.minimum(j, nu[0] - 1), 0)

    def w_map(j, be, bv, nu):
        return (layer, be[j], 0, 0)

    return pl.pallas_call(
        _ffn_kernel,
        grid_spec=pltpu.PrefetchScalarGridSpec(
            num_scalar_prefetch=3,
            grid=(p // tb,),
            in_specs=[
                pl.BlockSpec((tb, dw), row_map),
                pl.BlockSpec((1, 1, d, ff), w_map),
                pl.BlockSpec((1, 1, d, ff), w_map),
                pl.BlockSpec((1, 1, ff, d), w_map),
            ],
            out_specs=pl.BlockSpec((tb, dw), row_map),
            scratch_shapes=[pltpu.VMEM((d, ff), BF16), pltpu.VMEM((d, ff), BF16), pltpu.VMEM((ff, d), BF16)],
        ),
        out_shape=jax.ShapeDtypeStruct((p, dw), U32),
        compiler_params=_params("arbitrary"),
        name="ffn",
    )(blk_e, blk_valid, nused, xs, w_gate, w_up, w_down)


def _combine_kernel(dest_ref, x_ref, route_ref, g_ref, ys_ref, o_ref, buf_ref, sem, *, tm, top_k, final_norm):
    i = pl.program_id(0)
    slot = i % 2

    def gather_tile(tile, slot):
        base = tile * tm * top_k

        def issue(t, carry):
            for k in range(top_k):
                pltpu.make_async_copy(ys_ref.at[pl.ds(dest_ref[base + t * top_k + k], 1)],
                                      buf_ref.at[slot, k, pl.ds(t, 1)], sem.at[slot]).start()
            return carry

        lax.fori_loop(0, tm, issue, 0, unroll=DMA_ISSUE_UNROLL)

    @pl.when(i == 0)
    def _():
        gather_tile(i, slot)

    @pl.when(i + 1 < pl.num_programs(0))
    def _():
        gather_tile(i + 1, 1 - slot)

    for k in range(top_k):
        pltpu.make_async_copy(ys_ref.at[pl.ds(0, tm)], buf_ref.at[slot, k], sem.at[slot]).wait()

    route = route_ref[...]
    out = x_ref[...]
    for k in range(top_k):
        a, b = _unpack_bf16_pair(buf_ref[slot, k])
        out = out + route[:, 2 + k:3 + k] * jnp.concatenate([a, b], axis=1)
    if final_norm:
        ms = jnp.mean(out * out, axis=-1, keepdims=True)
        out = out * lax.rsqrt(ms + EPS) * g_ref[...]
    o_ref[...] = out


def combine(dest, x, route, g, ys, *, tm, top_k, final_norm):
    t, d = x.shape
    dw = ys.shape[1]
    return pl.pallas_call(
        functools.partial(_combine_kernel, tm=tm, top_k=top_k, final_norm=final_norm),
        grid_spec=pltpu.PrefetchScalarGridSpec(
            num_scalar_prefetch=1,
            grid=(t // tm,),
            in_specs=[
                pl.BlockSpec((tm, d), lambda i, dest: (i, 0)),
                pl.BlockSpec((tm, LANES), lambda i, dest: (i, 0)),
                pl.BlockSpec((1, d), lambda i, dest: (0, 0)),
                pl.BlockSpec(memory_space=pl.ANY),
            ],
            out_specs=pl.BlockSpec((tm, d), lambda i, dest: (i, 0)),
            scratch_shapes=[pltpu.VMEM((2, top_k, tm, dw), U32), pltpu.SemaphoreType.DMA((2,))],
        ),
        out_shape=jax.ShapeDtypeStruct((t, d), F32),
        compiler_params=_params("arbitrary"),
        name="combine",
    )(dest, x, route, g.reshape(1, d), ys)


def moe_layer(x_new, h_packed, route, counts, w_gate, w_up, w_down, g_final, *, layer, tb, tm_dispatch, tm_combine,
              final_norm):
    t = x_new.shape[0]
    top_k = 2
    cnt = counts[0, :MOE_EXPERTS].astype(I32)
    padded = ((cnt + tb - 1) // tb) * tb
    pends = jnp.cumsum(padded)
    pstarts = pends - padded
    expert_ids = jnp.arange(MOE_EXPERTS, dtype=I32)
    experts = route[:, 0:top_k].astype(I32)
    ranks = route[:, 4:4 + top_k].astype(I32)
    start_of = jnp.sum(jnp.where(experts[..., None] == expert_ids, pstarts, 0), axis=-1)
    dest = (start_of + ranks).reshape(t * top_k)
    rows_out = t * top_k + MOE_EXPERTS * tb
    nb = rows_out // tb
    blk_start = jnp.arange(nb, dtype=I32) * tb
    blk_e = jnp.minimum(jnp.sum((blk_start[:, None] >= pends[None, :]).astype(I32), axis=1), MOE_EXPERTS - 1)
    blk_onehot = blk_e[:, None] == expert_ids
    blk_end = jnp.sum(jnp.where(blk_onehot, pstarts + cnt, 0), axis=1)
    blk_valid = jnp.clip(blk_end - blk_start, 0, tb).astype(I32)
    nused = (pends[-1:] // tb).astype(I32)

    xs = dispatch(dest, h_packed, rows_out=rows_out, tm=tm_dispatch, top_k=top_k)
    ys = expert_ffn(blk_e, blk_valid, nused, xs, w_gate, w_up, w_down, layer=layer, tb=tb)
    return combine(dest, x_new, route, g_final, ys, tm=tm_combine, top_k=top_k, final_norm=final_norm)


def _rope_kernel(pos_ref, inv_ref, cos_ref, sin_ref):
    ang = pos_ref[...].astype(F32) * inv_ref[...]
    cos_ref[...] = jnp.cos(ang)
    sin_ref[...] = jnp.sin(ang)


def rope_table(positions, half, *, tm):
    t = positions.size
    inv_freq = (ROPE_BASE ** (-jnp.arange(half, dtype=F32) / half)).reshape(1, half)
    return pl.pallas_call(
        _rope_kernel,
        grid=(t // tm,),
        in_specs=[pl.BlockSpec((tm, 1), lambda i: (i, 0)), pl.BlockSpec((1, half), lambda i: (0, 0))],
        out_specs=[pl.BlockSpec((tm, half), lambda i: (i, 0))] * 2,
        out_shape=[jax.ShapeDtypeStruct((t, half), F32)] * 2,
        compiler_params=_params("arbitrary"),
        name="rope_table",
    )(positions.reshape(t, 1), inv_freq)


def _retention_kernel(lg_ref, q_ref, k_ref, v_ref, g_ref, cos_ref, sin_ref, gain_ref, o_ref,
                      state_ref, rhs_ref, kscale_ref, eps_ref, *, c, hp):
    dk = q_ref.shape[1] // hp
    dv = v_ref.shape[1] // hp
    half = dk // 2
    heads = range(hp)
    lgs = [lg_ref[pl.program_id(1) * hp + h] for h in heads]
    pos = lax.broadcasted_iota(I32, (c, dk), 0).astype(F32)

    @pl.when(pl.program_id(2) == 0)
    def _():
        state_ref[...] = jnp.zeros_like(state_ref)
        for h in heads:
            rhs_ref[h, c:, :] = jnp.zeros((dk, dv), BF16)
            kscale_ref[h] = jnp.exp(-(pos + 1.0) * lgs[h]) * (dk ** -0.5)
            eps_ref[h] = jnp.exp(-2.0 * (pos[:, 0:1] + 1.0) * lgs[h]) * (dv * EPS)

    cos = cos_ref[...]
    sin = sin_ref[...]

    def rotate(x):
        x1, x2 = x[:, :half], x[:, half:]
        return jnp.concatenate([x1 * cos - x2 * sin, x2 * cos + x1 * sin], axis=1)

    causal = lax.broadcasted_iota(I32, (c, c), 0) >= lax.broadcasted_iota(I32, (c, c), 1)

    def scores(h):
        qr = rotate(q_ref[:, h * dk:(h + 1) * dk].astype(F32)).astype(BF16)
        kp = (rotate(k_ref[:, h * dk:(h + 1) * dk].astype(F32)) * kscale_ref[h]).astype(BF16)
        s = lax.dot_general(qr, kp, (((1,), (1,)), ((), ())), preferred_element_type=F32)
        return qr, kp, s

    ahead = scores(0)
    for h in heads:
        qr, kp, s = ahead
        if h + 1 < hp:
            ahead = scores(h + 1)
        v = v_ref[:, h * dv:(h + 1) * dv]
        rhs_ref[h, :c, :] = v
        kv = lax.dot_general(kp, v, (((0,), (0,)), ((), ())), preferred_element_type=F32)
        s = jnp.where(causal, s, 0.0).astype(BF16)
        y = jnp.dot(jnp.concatenate([s, qr], axis=1), rhs_ref[h], preferred_element_type=F32)

        state = (state_ref[h] + kv) * jnp.exp(jnp.zeros((1, dv), F32) + c * lgs[h])
        state_ref[h] = state
        rhs_ref[h, c:, :] = state.astype(BF16)

        yc = y - jnp.sum(y, axis=-1, keepdims=True) * (1.0 / dv)
        ss = jnp.sum(yc * yc, axis=-1, keepdims=True)
        yn = yc * lax.rsqrt(ss + eps_ref[h]) * (gain_ref[0, h:h + 1, :] * (dv ** 0.5))
        g = g_ref[:, h * dv:(h + 1) * dv]
        gate = g / (1.0 + jnp.exp(-g))
        o_ref[:, h * dv:(h + 1) * dv] = yn.astype(o_ref.dtype) * gate


def retention(proj, cos, sin, gain, *, batch, seq, heads, dk, dv, c, hp):
    nchunk = seq // c
    log_g = jnp.log1p(-jnp.power(2.0, -5.0 - jnp.arange(heads, dtype=F32)))
    groups = heads // hp
    k_col = groups
    v_col = 2 * heads * dk // (hp * dv)
    g_col = v_col + groups
    half = dk // 2

    def rows(b, n):
        return b * nchunk + n

    return pl.pallas_call(
        functools.partial(_retention_kernel, c=c, hp=hp),
        grid_spec=pltpu.PrefetchScalarGridSpec(
            num_scalar_prefetch=1,
            grid=(batch, groups, nchunk),
            in_specs=[
                pl.BlockSpec((c, hp * dk), lambda b, h, n, lg: (rows(b, n), h)),
                pl.BlockSpec((c, hp * dk), lambda b, h, n, lg: (rows(b, n), k_col + h)),
                pl.BlockSpec((c, hp * dv), lambda b, h, n, lg: (rows(b, n), v_col + h)),
                pl.BlockSpec((c, hp * dv), lambda b, h, n, lg: (rows(b, n), g_col + h)),
                pl.BlockSpec((c, half), lambda b, h, n, lg: (rows(b, n), 0)),
                pl.BlockSpec((c, half), lambda b, h, n, lg: (rows(b, n), 0)),
                pl.BlockSpec((1, hp, dv), lambda b, h, n, lg: (h, 0, 0)),
            ],
            out_specs=pl.BlockSpec((c, hp * dv), lambda b, h, n, lg: (rows(b, n), h)),
            scratch_shapes=[
                pltpu.VMEM((hp, dk, dv), F32),
                pltpu.VMEM((hp, c + dk, dv), BF16),
                pltpu.VMEM((hp, c, dk), F32),
                pltpu.VMEM((hp, c, 1), F32),
            ],
        ),
        out_shape=jax.ShapeDtypeStruct((batch * seq, heads * dv), BF16),
        compiler_params=_params("arbitrary", "arbitrary", "arbitrary"),
        name="retention",
    )(log_g, proj, proj, proj, proj, cos, sin, gain.reshape(groups, hp, dv))


def _router_params(wg, bg, we, be):
    d = wg.shape[0]
    n = MOE_GROUPS + MOE_EXPERTS
    w = jnp.concatenate([wg, we, jnp.zeros((d, LANES - n), F32)], axis=1)
    b = jnp.concatenate([bg, be, jnp.zeros((LANES - n,), F32)]).reshape(1, LANES)
    w_hi = w.astype(BF16)
    w_lo = (w - w_hi.astype(F32)).astype(BF16)
    return jnp.concatenate([w_hi, w_lo], axis=1), b


def _pick(total, want):
    t = min(want, total)
    while total % t:
        t //= 2
    return t


def kernel(x, positions, ev_norm, ev_w_in, ev_conv_w, ev_conv_b, ev_gate_a_w, ev_gate_a_b, ev_gate_x_w, ev_gate_x_b, ev_lru_lambda, ev_w_out, od_norm, od_w_in, od_gn_gain, od_w_out, ffn_norm, router_group_w, router_group_b, router_expert_w, router_expert_b, expert_w_gate, expert_w_up, expert_w_down, final_norm):
    batch, seq, d = x.shape
    t = batch * seq
    depth = ffn_norm.shape[0]
    lru_w = ev_conv_w.shape[2]
    sb_w = (ev_w_in.shape[2] - 2 * lru_w) // 3
    head_dim = sb_w // SB_HEADS
    dv = od_gn_gain.shape[2]
    dk = (od_w_in.shape[2] - 2 * RET_HEADS * dv) // (2 * RET_HEADS)

    tm_proj = _pick(t, 1024)
    tm_out = _pick(t, 512)
    tm_dispatch = _pick(t, 2048)
    tm_combine = _pick(t, 512)
    tb = 512

    xf = x.reshape(t, d)
    for layer in range(depth):
        i = layer // 2
        if layer % 2 == 0:
            proj = norm_matmul(xf, ev_norm[i], ev_w_in[i].astype(BF16), tm=_pick(t, 512), tn=ev_w_in.shape[2])
            gate_w = jnp.concatenate([ev_gate_a_w[i], ev_gate_x_w[i]], axis=2).astype(BF16)
            gate_b = jnp.stack([ev_gate_a_b[i].reshape(lru_w), ev_gate_x_b[i].reshape(lru_w)])
            y_a = lru_branch(proj, ev_conv_w[i], ev_conv_b[i], gate_w, gate_b, ev_lru_lambda[i],
                             batch=batch, seq=seq, width=lru_w, tc=_pick(seq, 1024))
            col0 = 2 * lru_w // head_dim
            y_b = stick_breaking(proj, batch=batch, seq=seq, heads=SB_HEADS, head_dim=head_dim,
                                 q_col=col0, k_col=col0 + SB_HEADS, v_col=col0 + 2 * SB_HEADS,
                                 qb=SBA_QUERY_BLOCK, nchain=_pick(seq // SBA_QUERY_BLOCK, SBA_CHAINS),
                                 wk=SBA_FIRST_KEYS, fk=SBA_MORE_KEYS)
            w_out = ev_w_out[i].astype(BF16)
            ys, ws = [y_a, y_b], [w_out[:lru_w], w_out[lru_w:]]
        else:
            proj = norm_matmul(xf, od_norm[i], od_w_in[i].astype(BF16), tm=tm_proj, tn=_pick(od_w_in.shape[2], 2048))
            cos, sin = rope_table(positions, dk // 2, tm=_pick(t, 1024))
            y = retention(proj, cos, sin, od_gn_gain[i], batch=batch, seq=seq, heads=RET_HEADS,
                          dk=dk, dv=dv, c=_pick(seq, 256), hp=RET_HEADS_PER_STEP)
            ys, ws = [y], [od_w_out[i].astype(BF16)]

        rw, rb = _router_params(router_group_w[layer], router_group_b[layer],
                                router_expert_w[layer], router_expert_b[layer])
        x_new, h_packed, route, counts = out_router(ys, ws, xf, ffn_norm[layer], rw, rb, tm=tm_out)
        last = layer == depth - 1
        xf = moe_layer(x_new, h_packed, route, counts, expert_w_gate, expert_w_up, expert_w_down, final_norm,
                       layer=layer, tb=tb, tm_dispatch=tm_dispatch, tm_combine=tm_combine, final_norm=last)
    return xf.reshape(batch, seq, d)
```
